```python
import math
import jax, jax.numpy as jnp
from jax import lax
import numpy as np

D_MODEL = 1024
BATCH = 4
SEQ = 4096
DEPTH = 4

EPS = 1e-6
CHUNK = 64
CONV_K = 4
GDN_HEADS = 4
GDN_HEAD_DIM = 128
GDN_DIM = GDN_HEADS * GDN_HEAD_DIM
SSM_HEADS = 8
SSM_HEAD_DIM = 64
SSM_INNER = SSM_HEADS * SSM_HEAD_DIM
SSM_GROUPS = 2
SSM_STATE = 128
SSM_CONV_DIM = SSM_INNER + 2 * SSM_GROUPS * SSM_STATE
MLA_HEADS = 4
MLA_Q_LORA = 512
MLA_KV_LORA = 256
MLA_NOPE = 128
MLA_ROPE = 64
MLA_V = 128
MLA_DIM = MLA_HEADS * MLA_V
ROPE_THETA = 10000.0
Q_BLOCK = 128
N_BRANCH = 3
FFN_DIM = 2816
N_EXPERTS = 8
TOP_K = 2
EXPERT_DIM = 3584
MOE_BLOCK = 256
N_DENSE = (DEPTH + 1) // 2
N_MOE = DEPTH // 2

IN_SPLITS = (
    3 * GDN_DIM,
    GDN_DIM,
    GDN_HEADS,
    GDN_HEADS,
    SSM_INNER,
    SSM_CONV_DIM,
    SSM_HEADS,
    MLA_Q_LORA,
    MLA_KV_LORA,
    MLA_ROPE,
    N_BRANCH * D_MODEL,
)
IN_DIM = sum(IN_SPLITS)

kernel_name = 'hybrid_gdn_ssd_mla_moe_block'


def rms_norm(x, w=None):
    xf = x.astype(jnp.float32)
    y = xf * lax.rsqrt(jnp.mean(xf * xf, axis=-1, keepdims=True) + EPS)
    if w is not None:
        y = y * w.astype(jnp.float32)
    return y.astype(x.dtype)


def l2_normalize(x):
    return x * lax.rsqrt(jnp.sum(x * x, axis=-1, keepdims=True) + EPS)


def split_cols(y, sizes):
    idx = [int(i) for i in np.cumsum(sizes)[:-1]]
    return jnp.split(y, idx, axis=-1)


def causal_conv(x, w, b=None):
    k = w.shape[0]
    y = lax.conv_general_dilated(x, w[:, None, :].astype(x.dtype), window_strides=(1,),
                                 padding=[(k - 1, 0)], dimension_numbers=('NWC', 'WIO', 'NWC'),
                                 feature_group_count=x.shape[-1])
    if b is not None:
        y = y + b
    return y


def swiglu(x, w_gate, w_up, w_down):
    return (jax.nn.silu(x @ w_gate) * (x @ w_up)) @ w_down


def gated_delta_net(qkv, z, a, b, conv_w, a_log, dt_bias, norm_w):
    bsz, s, _ = qkv.shape
    H, Dh, C = GDN_HEADS, GDN_HEAD_DIM, CHUNK
    nc = s // C
    f32 = jnp.float32
    qkv = jax.nn.silu(causal_conv(qkv, conv_w)).astype(f32)
    q, k, v = jnp.split(qkv, 3, axis=-1)
    q = l2_normalize(q.reshape(bsz, s, H, Dh)) * (Dh ** -0.5)
    k = l2_normalize(k.reshape(bsz, s, H, Dh))
    v = v.reshape(bsz, s, H, Dh)
    beta = jax.nn.sigmoid(b.astype(f32))
    g = -jnp.exp(a_log.astype(f32)) * jax.nn.softplus(a.astype(f32) + dt_bias.astype(f32))

    def to_chunks(t):
        return t.reshape(bsz, nc, C, H, -1).transpose(0, 3, 1, 2, 4)
    q, k, v = to_chunks(q), to_chunks(k), to_chunks(v)
    beta = beta.reshape(bsz, nc, C, H).transpose(0, 3, 1, 2)
    g = g.reshape(bsz, nc, C, H).transpose(0, 3, 1, 2)
    G = jnp.cumsum(g, axis=-1)
    incl = jnp.tril(jnp.ones((C, C), dtype=bool))
    strict = jnp.tril(jnp.ones((C, C), dtype=bool), -1)
    decay = jnp.exp(jnp.where(incl, G[..., :, None] - G[..., None, :], -jnp.inf))
    kk = jnp.einsum('bhnid,bhnjd->bhnij', k, k)
    ia = jnp.eye(C, dtype=f32) + jnp.where(strict, beta[..., :, None] * kk * decay, 0.0)
    rhs = jnp.concatenate([v * beta[..., None], k * (beta * jnp.exp(G))[..., None]], axis=-1)
    sol = lax.linalg.triangular_solve(ia, rhs, left_side=True, lower=True, unit_diagonal=True)
    u, w = jnp.split(sol, 2, axis=-1)
    qk = jnp.einsum('bhnid,bhnjd->bhnij', q, k) * decay
    q_dec = q * jnp.exp(G)[..., None]
    k_dec = k * jnp.exp(G[..., -1:] - G)[..., None]
    g_last = jnp.exp(G[..., -1])

    def step(state, inp):
        qd, kd, uc, wc, qkc, gl = inp
        v_new = uc - jnp.einsum('bhcd,bhde->bhce', wc, state)
        o = jnp.einsum('bhcd,bhde->bhce', qd, state) + jnp.einsum('bhij,bhje->bhie', qkc, v_new)
        state = state * gl[..., None, None] + jnp.einsum('bhcd,bhce->bhde', kd, v_new)
        return state, o

    xs = tuple(jnp.moveaxis(t, 2, 0) for t in (q_dec, k_dec, u, w, qk, g_last))
    _, o = lax.scan(step, jnp.zeros((bsz, H, Dh, Dh), f32), xs)
    o = o.transpose(1, 0, 3, 2, 4).reshape(bsz, s, H, Dh)
    o = rms_norm(o, norm_w) * jax.nn.silu(z.astype(f32).reshape(bsz, s, H, Dh))
    return o.reshape(bsz, s, GDN_DIM).astype(z.dtype)


def mamba2_ssd(xbc, z, dt, conv_w, conv_b, a_log, dt_bias, d_skip, norm_w):
    bsz, s, _ = xbc.shape
    H, P, Gn, N, C = SSM_HEADS, SSM_HEAD_DIM, SSM_GROUPS, SSM_STATE, CHUNK
    E = H // Gn
    nc = s // C
    f32 = jnp.float32
    xbc = jax.nn.silu(causal_conv(xbc, conv_w, conv_b)).astype(f32)
    xs, bm, cm = jnp.split(xbc, [SSM_INNER, SSM_INNER + Gn * N], axis=-1)
    x = xs.reshape(bsz, s, H, P)
    dt = jax.nn.softplus(dt.astype(f32) + dt_bias.astype(f32))
    A = -jnp.exp(a_log.astype(f32))
    X = (x * dt[..., None]).reshape(bsz, nc, C, Gn, E, P)
    ad = (dt * A).reshape(bsz, nc, C, Gn, E).transpose(0, 3, 4, 1, 2)
    bm = bm.reshape(bsz, nc, C, Gn, N)
    cm = cm.reshape(bsz, nc, C, Gn, N)
    acs = jnp.cumsum(ad, axis=-1)
    incl = jnp.tril(jnp.ones((C, C), dtype=bool))
    L = jnp.exp(jnp.where(incl, acs[..., :, None] - acs[..., None, :], -jnp.inf))
    cb = jnp.einsum('bclgn,bcsgn->bcgls', cm, bm)
    y_diag = jnp.einsum('bcgls,bgecls,bcsgep->bclgep', cb, L, X)
    decay_states = jnp.exp(acs[..., -1:] - acs)
    states = jnp.einsum('bclgn,bgecl,bclgep->cbgepn', bm, decay_states, X)
    chunk_decay = jnp.moveaxis(jnp.exp(acs[..., -1]), -1, 0)

    def step(h, inp):
        s_c, d_c = inp
        return h * d_c[..., None, None] + s_c, h

    _, prev = lax.scan(step, jnp.zeros(states.shape[1:], f32), (states, chunk_decay))
    y_off = jnp.einsum('bclgn,cbgepn,bgecl->bclgep', cm, prev, jnp.exp(acs))
    y = (y_diag + y_off).reshape(bsz, s, H, P) + d_skip.astype(f32)[:, None] * x
    y = y.reshape(bsz, s, Gn, SSM_INNER // Gn) * jax.nn.silu(z.astype(f32).reshape(bsz, s, Gn, -1))
    y = rms_norm(y, norm_w.reshape(Gn, -1))
    return y.reshape(bsz, s, SSM_INNER).astype(z.dtype)


def apply_rope(x, cos, sin):
    x1, x2 = jnp.split(x, 2, axis=-1)
    return jnp.concatenate([x1 * cos - x2 * sin, x2 * cos + x1 * sin], axis=-1)


def mla_attention(c_q, c_kv, k_r, positions, q_norm_w, w_uq, kv_norm_w, w_uk, w_uv):
    bsz, s, _ = c_q.shape
    H = MLA_HEADS
    f32 = jnp.float32
    q = (rms_norm(c_q, q_norm_w) @ w_uq).reshape(bsz, s, H, MLA_NOPE + MLA_ROPE)
    q_nope, q_rope = q[..., :MLA_NOPE], q[..., MLA_NOPE:]
    ckv = rms_norm(c_kv, kv_norm_w)
    k_nope = (ckv @ w_uk).reshape(bsz, s, H, MLA_NOPE)
    v = (ckv @ w_uv).reshape(bsz, s, H, MLA_V)
    inv_freq = ROPE_THETA ** (-jnp.arange(0, MLA_ROPE, 2, dtype=f32) / MLA_ROPE)
    ang = positions.astype(f32)[..., None] * inv_freq
    cos, sin = jnp.cos(ang), jnp.sin(ang)
    q_rope = apply_rope(q_rope.astype(f32), cos[:, :, None], sin[:, :, None])
    k_rope = apply_rope(k_r.astype(f32), cos, sin)
    scale = (MLA_NOPE + MLA_ROPE) ** -0.5
    qn = q_nope.transpose(0, 2, 1, 3)
    qr = q_rope.transpose(0, 2, 1, 3)
    kn = k_nope.transpose(0, 2, 1, 3)
    vh = v.transpose(0, 2, 1, 3)
    key_idx = jnp.arange(s)

    def attend_block(i):
        start = i * Q_BLOCK
        qn_b = lax.dynamic_slice_in_dim(qn, start, Q_BLOCK, axis=2)
        qr_b = lax.dynamic_slice_in_dim(qr, start, Q_BLOCK, axis=2)
        sc = (jnp.einsum('bhqd,bhkd->bhqk', qn_b, kn).astype(f32)
              + jnp.einsum('bhqr,bkr->bhqk', qr_b, k_rope)) * scale
        causal = (start + jnp.arange(Q_BLOCK))[:, None] >= key_idx[None, :]
        p = jax.nn.softmax(jnp.where(causal, sc, -jnp.inf), axis=-1)
        return jnp.einsum('bhqk,bhkd->bhqd', p.astype(vh.dtype), vh)

    o = lax.map(attend_block, jnp.arange(s // Q_BLOCK))
    return o.transpose(1, 0, 3, 2, 4).reshape(bsz, s, MLA_DIM)


def token_mixer(h, positions, w_in, gdn_conv_w, gdn_a_log, gdn_dt_bias, gdn_norm_w,
                ssm_conv_w, ssm_conv_b, ssm_a_log, ssm_dt_bias, ssm_d, ssm_norm_w,
                mla_q_norm_w, mla_w_uq, mla_kv_norm_w, mla_w_uk, mla_w_uv,
                w_branch_a, w_branch_b, w_branch_c, w_out):
    (gdn_qkv, gdn_z, gdn_a, gdn_b, ssm_z, ssm_xbc, ssm_dt,
     mla_cq, mla_ckv, mla_kr, gates) = split_cols(h @ w_in, IN_SPLITS)
    y_a = gated_delta_net(gdn_qkv, gdn_z, gdn_a, gdn_b, gdn_conv_w, gdn_a_log, gdn_dt_bias, gdn_norm_w)
    y_b = mamba2_ssd(ssm_xbc, ssm_z, ssm_dt, ssm_conv_w, ssm_conv_b, ssm_a_log, ssm_dt_bias, ssm_d, ssm_norm_w)
    y_c = mla_attention(mla_cq, mla_ckv, mla_kr, positions, mla_q_norm_w, mla_w_uq,
                        mla_kv_norm_w, mla_w_uk, mla_w_uv)
    g_a, g_b, g_c = jnp.split(jax.nn.sigmoid(gates), N_BRANCH, axis=-1)
    merged = g_a * (y_a @ w_branch_a) + g_b * (y_b @ w_branch_b) + g_c * (y_c @ w_branch_c)
    return merged @ w_out


def moe_ffn(h, router, w_gate, w_up, w_down):
    bsz, s, d = h.shape
    T = bsz * s
    f32 = jnp.float32
    xt = h.reshape(T, d)
    logits = (xt @ router).astype(f32)
    top_logits, top_idx = lax.top_k(logits, TOP_K)
    top_w = jax.nn.softmax(top_logits, axis=-1)
    n_assign = T * TOP_K
    n_blocks = (n_assign + N_EXPERTS * (MOE_BLOCK - 1)) // MOE_BLOCK
    slots = n_blocks * MOE_BLOCK
    flat_e = top_idx.reshape(-1)
    flat_tok = jnp.repeat(jnp.arange(T, dtype=jnp.int32), TOP_K)
    flat_w = top_w.reshape(-1)
    order = jnp.argsort(flat_e)
    sorted_e = flat_e[order]
    counts = jnp.bincount(flat_e, length=N_EXPERTS)
    padded = (counts + MOE_BLOCK - 1) // MOE_BLOCK * MOE_BLOCK
    start_sorted = jnp.cumsum(counts) - counts
    pad_end = jnp.cumsum(padded)
    start_padded = pad_end - padded
    dest = start_padded[sorted_e] + jnp.arange(n_assign) - start_sorted[sorted_e]
    slot_tok = jnp.full((slots,), T, jnp.int32).at[dest].set(flat_tok[order])
    slot_w = jnp.zeros((slots,), f32).at[dest].set(flat_w[order])
    block_e = jnp.minimum(jnp.searchsorted(pad_end, jnp.arange(n_blocks) * MOE_BLOCK, side='right'),
                          N_EXPERTS - 1)
    x_slots = jnp.take(xt, slot_tok, axis=0, mode='fill', fill_value=0).reshape(n_blocks, MOE_BLOCK, d)

    def expert_block(args):
        xb, e = args
        return swiglu(xb, w_gate[e], w_up[e], w_down[e])

    y_slots = lax.map(expert_block, (x_slots, block_e)).reshape(slots, d)
    out = jnp.zeros((T, d), f32).at[slot_tok].add(y_slots.astype(f32) * slot_w[:, None], mode='drop')
    return out.astype(h.dtype).reshape(bsz, s, d)


def setup_inputs(seed: int = 0) -> dict:
    key = jax.random.key(seed)
    ks = list(jax.random.split(key, 48))
    f32 = jnp.float32
    L = DEPTH

    def normal(shape, scale):
        return jax.random.normal(ks.pop(), shape, f32) * scale

    def uniform(shape, lo, hi):
        return jax.random.uniform(ks.pop(), shape, f32, lo, hi)

    def dt_bias_init(shape):
        dt = jnp.exp(uniform(shape, math.log(1e-3), math.log(1e-1)))
        return jnp.log(jnp.expm1(dt))

    x = normal((BATCH, SEQ, D_MODEL), 1.0)
    c = normal((BATCH, D_MODEL), 1.0)
    positions = (jax.random.randint(ks.pop(), (BATCH, 1), 0, 1024, dtype=jnp.int32)
                 + jnp.arange(SEQ, dtype=jnp.int32)[None, :])
    return {
        'x': x,
        'c': c,
        'positions': positions,
        'w_ada': normal((L, D_MODEL, 6 * D_MODEL), 0.5 * D_MODEL ** -0.5),
        'b_ada': normal((L, 6 * D_MODEL), 0.02),
        'w_in': normal((L, D_MODEL, IN_DIM), D_MODEL ** -0.5),
        'gdn_conv_w': normal((L, CONV_K, 3 * GDN_DIM), CONV_K ** -0.5),
        'gdn_a_log': jnp.log(uniform((L, GDN_HEADS), 1.0, 16.0)),
        'gdn_dt_bias': dt_bias_init((L, GDN_HEADS)),
        'gdn_norm_w': 1.0 + normal((L, GDN_HEAD_DIM), 0.02),
        'ssm_conv_w': normal((L, CONV_K, SSM_CONV_DIM), CONV_K ** -0.5),
        'ssm_conv_b': normal((L, SSM_CONV_DIM), 0.02),
        'ssm_a_log': jnp.log(uniform((L, SSM_HEADS), 1.0, 16.0)),
        'ssm_dt_bias': dt_bias_init((L, SSM_HEADS)),
        'ssm_d': 1.0 + normal((L, SSM_HEADS), 0.1),
        'ssm_norm_w': 1.0 + normal((L, SSM_INNER), 0.02),
        'mla_q_norm_w': 1.0 + normal((L, MLA_Q_LORA), 0.02),
        'mla_w_uq': normal((L, MLA_Q_LORA, MLA_HEADS * (MLA_NOPE + MLA_ROPE)), MLA_Q_LORA ** -0.5),
        'mla_kv_norm_w': 1.0 + normal((L, MLA_KV_LORA), 0.02),
        'mla_w_uk': normal((L, MLA_KV_LORA, MLA_HEADS * MLA_NOPE), MLA_KV_LORA ** -0.5),
        'mla_w_uv': normal((L, MLA_KV_LORA, MLA_HEADS * MLA_V), MLA_KV_LORA ** -0.5),
        'w_branch_a': normal((L, GDN_DIM, D_MODEL), GDN_DIM ** -0.5),
        'w_branch_b': normal((L, SSM_INNER, D_MODEL), SSM_INNER ** -0.5),
        'w_branch_c': normal((L, MLA_DIM, D_MODEL), MLA_DIM ** -0.5),
        'w_out': normal((L, D_MODEL, D_MODEL), D_MODEL ** -0.5),
        'ffn_w_gate': normal((N_DENSE, D_MODEL, FFN_DIM), D_MODEL ** -0.5),
        'ffn_w_up': normal((N_DENSE, D_MODEL, FFN_DIM), D_MODEL ** -0.5),
        'ffn_w_down': normal((N_DENSE, FFN_DIM, D_MODEL), FFN_DIM ** -0.5),
        'moe_router': normal((N_MOE, D_MODEL, N_EXPERTS), D_MODEL ** -0.5),
        'moe_w_gate': normal((N_MOE, N_EXPERTS, D_MODEL, EXPERT_DIM), D_MODEL ** -0.5),
        'moe_w_up': normal((N_MOE, N_EXPERTS, D_MODEL, EXPERT_DIM), D_MODEL ** -0.5),
        'moe_w_down': normal((N_MOE, N_EXPERTS, EXPERT_DIM, D_MODEL), EXPERT_DIM ** -0.5),
        'final_norm_w': 1.0 + normal((D_MODEL,), 0.02),
    }


def reference(x, c, positions, w_ada, b_ada, w_in, gdn_conv_w, gdn_a_log, gdn_dt_bias, gdn_norm_w,
              ssm_conv_w, ssm_conv_b, ssm_a_log, ssm_dt_bias, ssm_d, ssm_norm_w,
              mla_q_norm_w, mla_w_uq, mla_kv_norm_w, mla_w_uk, mla_w_uv,
              w_branch_a, w_branch_b, w_branch_c, w_out,
              ffn_w_gate, ffn_w_up, ffn_w_down,
              moe_router, moe_w_gate, moe_w_up, moe_w_down, final_norm_w):
    cond = jax.nn.silu(c)
    for l in range(DEPTH):
        mod = cond @ w_ada[l] + b_ada[l]
        sh1, sc1, g1, sh2, sc2, g2 = [m[:, None, :] for m in jnp.split(mod, 6, axis=-1)]
        h = rms_norm(x) * (1 + sc1) + sh1
        x = x + g1 * token_mixer(h, positions, w_in[l], gdn_conv_w[l], gdn_a_log[l], gdn_dt_bias[l],
                                 gdn_norm_w[l], ssm_conv_w[l], ssm_conv_b[l], ssm_a_log[l],
                                 ssm_dt_bias[l], ssm_d[l], ssm_norm_w[l], mla_q_norm_w[l],
                                 mla_w_uq[l], mla_kv_norm_w[l], mla_w_uk[l], mla_w_uv[l],
                                 w_branch_a[l], w_branch_b[l], w_branch_c[l], w_out[l])
        h = rms_norm(x) * (1 + sc2) + sh2
        i = l // 2
        if l % 2 == 0:
            f = swiglu(h, ffn_w_gate[i], ffn_w_up[i], ffn_w_down[i])
        else:
            f = moe_ffn(h, moe_router[i], moe_w_gate[i], moe_w_up[i], moe_w_down[i])
        x = x + g2 * f
    return rms_norm(x, final_norm_w)
```

```python
import functools
import math

import numpy as np
import jax
import jax.numpy as jnp
from jax import lax
from jax.experimental import pallas as pl
from jax.experimental.pallas import tpu as pltpu

F32 = jnp.float32
BF16 = jnp.bfloat16
HI = lax.Precision.HIGHEST

EPS = 1e-6
CHUNK = 64
CONV_K = 4
GDN_HEADS = 4
GDN_HEAD_DIM = 128
GDN_DIM = GDN_HEADS * GDN_HEAD_DIM
SSM_HEADS = 8
SSM_HEAD_DIM = 64
SSM_INNER = SSM_HEADS * SSM_HEAD_DIM
SSM_GROUPS = 2
SSM_STATE = 128
SSM_CONV_DIM = SSM_INNER + 2 * SSM_GROUPS * SSM_STATE
MLA_HEADS = 4
MLA_Q_LORA = 512
MLA_KV_LORA = 256
MLA_NOPE = 128
MLA_ROPE = 64
MLA_V = 128
MLA_DIM = MLA_HEADS * MLA_V
ROPE_THETA = 10000.0
N_EXPERTS = 8
TOP_K = 2
MOE_BLOCK = 256

LANES = 128
SUBLANES = 8
VMEM_LIMIT = 56 * 1024 * 1024

COL_QKV = 0
COL_GZ = 1536
COL_SZ = 2048
COL_CQ = 2560
COL_XBC = 3072
COL_GATES = 4096
COL_CKV = 7168
COL_SMALL = 7424
COL_KR = 7552
IN_COLS = 7680


def _cparams(*sem):
    return pltpu.CompilerParams(dimension_semantics=tuple(sem), vmem_limit_bytes=VMEM_LIMIT)


def _silu(x):
    return x * (1.0 / (1.0 + jnp.exp(-x)))


def _sigmoid(x):
    return 1.0 / (1.0 + jnp.exp(-x))


def _softplus(x):
    return jnp.maximum(x, 0.0) + jnp.log(1.0 + jnp.exp(-jnp.abs(x)))


def _rms(x):
    return x * lax.rsqrt(jnp.mean(x * x, axis=-1, keepdims=True) + EPS)


def _dot(a, b):
    return jnp.dot(a.astype(BF16), b.astype(BF16), preferred_element_type=F32)


def _dot_nt(a, b):
    return lax.dot_general(a.astype(BF16), b.astype(BF16), (((1,), (1,)), ((), ())),
                           preferred_element_type=F32)


def _dot_tn(a, b):
    return lax.dot_general(a.astype(BF16), b.astype(BF16), (((0,), (0,)), ((), ())),
                           preferred_element_type=F32)


def _dot_hi(a, b):
    return jnp.dot(a, b, precision=HI, preferred_element_type=F32)


def _mod_kernel(c_ref, w_ref, b_ref, o_ref):
    cond = _silu(c_ref[...])
    o_ref[...] = _dot(cond, w_ref[...]) + b_ref[...]


def _modulation(c_pad, w_ada, b_ada):
    L, D, N = w_ada.shape
    tn = 1536
    return pl.pallas_call(
        _mod_kernel,
        out_shape=jax.ShapeDtypeStruct((L, SUBLANES, N), F32),
        grid=(L, N // tn),
        in_specs=[pl.BlockSpec((SUBLANES, D), lambda l, j: (0, 0)),
                  pl.BlockSpec((None, D, tn), lambda l, j: (l, 0, j)),
                  pl.BlockSpec((None, 1, tn), lambda l, j: (l, 0, j))],
        out_specs=pl.BlockSpec((None, SUBLANES, tn), lambda l, j: (l, 0, j)),
        compiler_params=_cparams("parallel", "parallel"),
        name="adaln_mod",
    )(c_pad, w_ada, b_ada.reshape(L, 1, N))


def _rope_table_kernel(pos_ref, freq_ref, cos_ref, sin_ref):
    ang = pos_ref[...].astype(F32) * freq_ref[...]
    keep = lax.broadcasted_iota(jnp.int32, ang.shape, 1) < MLA_ROPE
    cos_ref[...] = jnp.where(keep, jnp.cos(ang), 0.0)
    sin_ref[...] = jnp.where(keep, jnp.sin(ang), 0.0)


def _rope_tables(positions):
    T = positions.size
    tm = min(T, 1024)
    inv_freq = ROPE_THETA ** (-jnp.arange(0, MLA_ROPE, 2, dtype=F32) / MLA_ROPE)
    freq = jnp.concatenate([inv_freq, inv_freq, jnp.zeros((LANES - MLA_ROPE,), F32)]).reshape(1, LANES)
    return pl.pallas_call(
        _rope_table_kernel,
        out_shape=(jax.ShapeDtypeStruct((T, LANES), F32), jax.ShapeDtypeStruct((T, LANES), F32)),
        grid=(T // tm,),
        in_specs=[pl.BlockSpec((tm, 1), lambda i: (i, 0)),
                  pl.BlockSpec((1, LANES), lambda i: (0, 0))],
        out_specs=(pl.BlockSpec((tm, LANES), lambda i: (i, 0)),
                   pl.BlockSpec((tm, LANES), lambda i: (i, 0))),
        compiler_params=_cparams("parallel"),
        name="rope_tables",
    )(positions.reshape(T, 1), freq)


def _in_proj_kernel(x_ref, sc_ref, sh_ref, w_ref, o_ref, h_ref):
    @pl.when(pl.program_id(2) == 0)
    def _():
        h = _rms(x_ref[...]) * (1.0 + sc_ref[...]) + sh_ref[...]
        h_ref[...] = h.astype(BF16)

    o_ref[...] = jnp.dot(h_ref[...], w_ref[...], preferred_element_type=F32)


def _in_proj(x, sc, sh, w):
    B, S, D = x.shape
    N = w.shape[1]
    tm = min(S, 1024)
    tn = 1536
    return pl.pallas_call(
        _in_proj_kernel,
        out_shape=jax.ShapeDtypeStruct((B, S, N), F32),
        grid=(B, S // tm, N // tn),
        in_specs=[pl.BlockSpec((None, tm, D), lambda b, i, j: (b, i, 0)),
                  pl.BlockSpec((None, 1, D), lambda b, i, j: (b, 0, 0)),
                  pl.BlockSpec((None, 1, D), lambda b, i, j: (b, 0, 0)),
                  pl.BlockSpec((D, tn), lambda b, i, j: (0, j))],
        out_specs=pl.BlockSpec((None, tm, tn), lambda b, i, j: (b, i, j)),
        scratch_shapes=[pltpu.VMEM((tm, D), BF16)],
        compiler_params=_cparams("parallel", "parallel", "arbitrary"),
        name="in_proj",
    )(x, sc, sh, w)


def _causal_conv(x, tail_ref, w):
    rows = x.shape[0]
    xe = jnp.concatenate([tail_ref[...], x], axis=0)
    tail_ref[...] = x[rows - SUBLANES:, :]
    y = xe[SUBLANES:, :] * w[CONV_K - 1:CONV_K, :]
    for j in range(CONV_K - 1):
        shift = CONV_K - 1 - j
        y = y + pltpu.roll(xe, shift, axis=0)[SUBLANES:, :] * w[j:j + 1, :]
    return y


def _chunk_cumsum(v):
    rows = v.shape[0]
    pos = lax.broadcasted_iota(jnp.int32, v.shape, 0) % CHUNK
    step = 1
    while step < CHUNK:
        v = v + jnp.where(pos >= step, pltpu.roll(v, step, axis=0), 0.0)
        step *= 2
    return v


def _col_to_row(col_b, eye):
    return jnp.sum(jnp.where(eye, col_b, 0.0), axis=0, keepdims=True)


def _gdn_kernel(qkv_ref, z_ref, sm_ref, cw_ref, par_ref, nw_ref, o_ref,
                tail_ref, state_ref, *, rows):
    C, H, Dh = CHUNK, GDN_HEADS, GDN_HEAD_DIM

    @pl.when(pl.program_id(1) == 0)
    def _():
        tail_ref[...] = jnp.zeros_like(tail_ref)
        state_ref[...] = jnp.zeros_like(state_ref)

    y = _silu(_causal_conv(qkv_ref[...], tail_ref, cw_ref[...]))
    sm = sm_ref[...]
    neg_a = par_ref[0:1, :]
    dt_b = par_ref[1:2, :]
    g_all = _chunk_cumsum(neg_a * _softplus(sm + dt_b))
    beta_all = _sigmoid(sm)
    z = z_ref[...]
    nw = nw_ref[...]

    ri = lax.broadcasted_iota(jnp.int32, (C, C), 0)
    ci = lax.broadcasted_iota(jnp.int32, (C, C), 1)
    eye = ri == ci
    incl = ri >= ci
    strict = ri > ci
    eye_f = jnp.where(eye, 1.0, 0.0)

    for h in range(H):
        q_h = y[:, h * Dh:(h + 1) * Dh]
        k_h = y[:, GDN_DIM + h * Dh:GDN_DIM + (h + 1) * Dh]
        v_h = y[:, 2 * GDN_DIM + h * Dh:2 * GDN_DIM + (h + 1) * Dh]
        q_h = q_h * lax.rsqrt(jnp.sum(q_h * q_h, axis=-1, keepdims=True) + EPS) * (Dh ** -0.5)
        k_h = k_h * lax.rsqrt(jnp.sum(k_h * k_h, axis=-1, keepdims=True) + EPS)
        state = state_ref[h]
        for c in range(rows // C):
            r0 = c * C
            q_c, k_c, v_c = q_h[r0:r0 + C], k_h[r0:r0 + C], v_h[r0:r0 + C]
            g_col = g_all[r0:r0 + C, h:h + 1]
            beta = beta_all[r0:r0 + C, H + h:H + h + 1]
            g_b = jnp.broadcast_to(g_col, (C, C))
            g_row = _col_to_row(g_b, eye)
            g_last = g_col[C - 1:C, :]
            decay = jnp.exp(jnp.where(incl, g_b - g_row, -1e30))
            kk = _dot_nt(k_c, k_c)
            p = jnp.where(strict, -(beta * kk * decay), 0.0)
            inv = eye_f + p
            for _ in range(5):
                p = _dot_hi(p, p)
                inv = inv + _dot_hi(inv, p)
            eg = jnp.exp(g_col)
            rhs = jnp.concatenate([v_c * beta, k_c * (beta * eg)], axis=-1)
            sol = _dot_hi(inv, rhs)
            u, w = sol[:, :Dh], sol[:, Dh:]
            qk = _dot_nt(q_c, k_c) * decay
            v_new = u - _dot(w, state)
            o = _dot(q_c * eg, state) + _dot(qk, v_new)
            k_dec = k_c * jnp.exp(g_last - g_col)
            state = state * jnp.exp(g_last) + _dot_tn(k_dec, v_new)
            z_c = z[r0:r0 + C, h * Dh:(h + 1) * Dh]
            o_ref[r0:r0 + C, h * Dh:(h + 1) * Dh] = _rms(o) * nw * _silu(z_c)
        state_ref[h] = state


def _gdn(proj, conv_w, a_log, dt_bias, norm_w):
    B, S, _ = proj.shape
    rows = min(S, 256)
    par = jnp.zeros((SUBLANES, LANES), F32)
    par = par.at[0, :GDN_HEADS].set(-jnp.exp(a_log)).at[1, :GDN_HEADS].set(dt_bias)
    w3 = 3 * GDN_DIM
    return pl.pallas_call(
        functools.partial(_gdn_kernel, rows=rows),
        out_shape=jax.ShapeDtypeStruct((B, S, GDN_DIM), F32),
        grid=(B, S // rows),
        in_specs=[pl.BlockSpec((None, rows, w3), lambda b, i: (b, i, COL_QKV // w3)),
                  pl.BlockSpec((None, rows, GDN_DIM), lambda b, i: (b, i, COL_GZ // GDN_DIM)),
                  pl.BlockSpec((None, rows, LANES), lambda b, i: (b, i, COL_SMALL // LANES)),
                  pl.BlockSpec((CONV_K, w3), lambda b, i: (0, 0)),
                  pl.BlockSpec((SUBLANES, LANES), lambda b, i: (0, 0)),
                  pl.BlockSpec((1, GDN_HEAD_DIM), lambda b, i: (0, 0))],
        out_specs=pl.BlockSpec((None, rows, GDN_DIM), lambda b, i: (b, i, 0)),
        scratch_shapes=[pltpu.VMEM((SUBLANES, w3), F32),
                        pltpu.VMEM((GDN_HEADS, GDN_HEAD_DIM, GDN_HEAD_DIM), F32)],
        compiler_params=_cparams("parallel", "arbitrary"),
        name="gated_delta_net",
    )(proj, proj, proj, conv_w, par, norm_w.reshape(1, GDN_HEAD_DIM))


def _ssd_kernel(xbc_ref, z_ref, sm_ref, cw_ref, cb_ref, par_ref, nw_ref, o_ref,
                tail_ref, state_ref, *, rows):
    C, H, P, G, N = CHUNK, SSM_HEADS, SSM_HEAD_DIM, SSM_GROUPS, SSM_STATE
    E = H // G
    GW = E * P

    @pl.when(pl.program_id(1) == 0)
    def _():
        tail_ref[...] = jnp.zeros_like(tail_ref)
        state_ref[...] = jnp.zeros_like(state_ref)

    y = _silu(_causal_conv(xbc_ref[...], tail_ref, cw_ref[...]) + cb_ref[...])
    sm = sm_ref[...]
    neg_a = par_ref[0:1, :]
    dt_b = par_ref[1:2, :]
    d_skip = par_ref[2:3, :]
    dt_all = _softplus(sm + dt_b)
    acs_all = _chunk_cumsum(dt_all * neg_a)
    z = z_ref[...]
    nw = nw_ref[...]

    ri = lax.broadcasted_iota(jnp.int32, (C, C), 0)
    ci = lax.broadcasted_iota(jnp.int32, (C, C), 1)
    eye = ri == ci
    incl = ri >= ci

    for g in range(G):
        bm_g = y[:, SSM_INNER + g * N:SSM_INNER + (g + 1) * N]
        cm_g = y[:, SSM_INNER + G * N + g * N:SSM_INNER + G * N + (g + 1) * N]
        x_g = y[:, g * GW:(g + 1) * GW]
        state = state_ref[g]
        for c in range(rows // C):
            r0 = c * C
            bm_c, cm_c, x_c = bm_g[r0:r0 + C], cm_g[r0:r0 + C], x_g[r0:r0 + C]
            cb = _dot_nt(cm_c, bm_c)
            y_off = _dot(cm_c, state)
            xs_parts, ds_parts, ea_parts, ydiag_parts, dx_parts = [], [], [], [], []
            last_parts = []
            for e in range(E):
                lane = SUBLANES + g * E + e
                a_col = acs_all[r0:r0 + C, lane:lane + 1]
                dt_col = dt_all[r0:r0 + C, lane:lane + 1]
                a_b = jnp.broadcast_to(a_col, (C, C))
                a_row = _col_to_row(a_b, eye)
                a_last = a_col[C - 1:C, :]
                L = jnp.exp(jnp.where(incl, a_b - a_row, -1e30))
                x_e = x_c[:, e * P:(e + 1) * P]
                xd = x_e * dt_col
                ydiag_parts.append(_dot(cb * L, xd))
                xs_parts.append(xd * jnp.exp(a_last - a_col))
                ea_parts.append(jnp.broadcast_to(jnp.exp(a_col), (C, P)))
                last_parts.append(jnp.broadcast_to(jnp.exp(a_last), (1, P)))
                dx_parts.append(x_e * d_skip[:, lane:lane + 1])
            y_c = (jnp.concatenate(ydiag_parts, axis=-1)
                   + y_off * jnp.concatenate(ea_parts, axis=-1)
                   + jnp.concatenate(dx_parts, axis=-1))
            state = (state * jnp.concatenate(last_parts, axis=-1)
                     + _dot_tn(bm_c, jnp.concatenate(xs_parts, axis=-1)))
            y_c = y_c * _silu(z[r0:r0 + C, g * GW:(g + 1) * GW])
            o_ref[r0:r0 + C, g * GW:(g + 1) * GW] = _rms(y_c) * nw[:, g * GW:(g + 1) * GW]
        state_ref[g] = state


def _ssd(proj, conv_w, conv_b, a_log, dt_bias, d_skip, norm_w):
    B, S, _ = proj.shape
    rows = min(S, 256)
    par = jnp.zeros((SUBLANES, LANES), F32)
    sl = slice(SUBLANES, SUBLANES + SSM_HEADS)
    par = par.at[0, sl].set(-jnp.exp(a_log)).at[1, sl].set(dt_bias).at[2, sl].set(d_skip)
    return pl.pallas_call(
        functools.partial(_ssd_kernel, rows=rows),
        out_shape=jax.ShapeDtypeStruct((B, S, SSM_INNER), F32),
        grid=(B, S // rows),
        in_specs=[pl.BlockSpec((None, rows, SSM_CONV_DIM), lambda b, i: (b, i, COL_XBC // SSM_CONV_DIM)),
                  pl.BlockSpec((None, rows, SSM_INNER), lambda b, i: (b, i, COL_SZ // SSM_INNER)),
                  pl.BlockSpec((None, rows, LANES), lambda b, i: (b, i, COL_SMALL // LANES)),
                  pl.BlockSpec((CONV_K, SSM_CONV_DIM), lambda b, i: (0, 0)),
                  pl.BlockSpec((1, SSM_CONV_DIM), lambda b, i: (0, 0)),
                  pl.BlockSpec((SUBLANES, LANES), lambda b, i: (0, 0)),
                  pl.BlockSpec((1, SSM_INNER), lambda b, i: (0, 0))],
        out_specs=pl.BlockSpec((None, rows, SSM_INNER), lambda b, i: (b, i, 0)),
        scratch_shapes=[pltpu.VMEM((SUBLANES, SSM_CONV_DIM), F32),
                        pltpu.VMEM((SSM_GROUPS, SSM_STATE, SSM_INNER // SSM_GROUPS), F32)],
        compiler_params=_cparams("parallel", "arbitrary"),
        name="mamba2_ssd",
    )(proj, proj, proj, conv_w, conv_b.reshape(1, -1), par, norm_w.reshape(1, -1))


QK_W = 2 * LANES


def _rope(x, cos, sin):
    return x * cos + pltpu.roll(x, MLA_ROPE, axis=1) * sin


def _mla_proj_kernel(cq_ref, ckv_ref, kr_ref, cos_ref, sin_ref, qw_ref, wq_ref, kvw_ref, wk_ref, wv_ref,
                     q_ref, k_ref, v_ref):
    scale = (MLA_NOPE + MLA_ROPE) ** -0.5
    cos, sin = cos_ref[...], sin_ref[...]
    qc = _rms(cq_ref[...]) * qw_ref[...]
    qf = _dot(qc, wq_ref[...]) * scale
    ckv = (_rms(ckv_ref[...]) * kvw_ref[...]).astype(BF16)
    kn = jnp.dot(ckv, wk_ref[...], preferred_element_type=F32)
    v_ref[...] = jnp.dot(ckv, wv_ref[...], preferred_element_type=F32).astype(BF16)
    kr = _rope(kr_ref[...], cos, sin).astype(BF16)
    for h in range(MLA_HEADS):
        c0 = h * QK_W
        q_ref[:, c0:c0 + LANES] = qf[:, c0:c0 + LANES].astype(BF16)
        q_ref[:, c0 + LANES:c0 + QK_W] = _rope(qf[:, c0 + LANES:c0 + QK_W], cos, sin).astype(BF16)
        k_ref[:, c0:c0 + LANES] = kn[:, h * MLA_NOPE:(h + 1) * MLA_NOPE].astype(BF16)
        k_ref[:, c0 + LANES:c0 + QK_W] = kr


def _mla_proj(proj2d, cos, sin, q_norm_w, wq, kv_norm_w, wk, wv):
    T = proj2d.shape[0]
    tm = min(T, 512)
    HQ = MLA_HEADS * QK_W
    row = lambda i: (i, 0)
    const = lambda i: (0, 0)
    return pl.pallas_call(
        _mla_proj_kernel,
        out_shape=(jax.ShapeDtypeStruct((T, HQ), BF16), jax.ShapeDtypeStruct((T, HQ), BF16),
                   jax.ShapeDtypeStruct((T, MLA_DIM), BF16)),
        grid=(T // tm,),
        in_specs=[pl.BlockSpec((tm, MLA_Q_LORA), lambda i: (i, COL_CQ // MLA_Q_LORA)),
                  pl.BlockSpec((tm, MLA_KV_LORA), lambda i: (i, COL_CKV // MLA_KV_LORA)),
                  pl.BlockSpec((tm, LANES), lambda i: (i, COL_KR // LANES)),
                  pl.BlockSpec((tm, LANES), row),
                  pl.BlockSpec((tm, LANES), row),
                  pl.BlockSpec((1, MLA_Q_LORA), const),
                  pl.BlockSpec((MLA_Q_LORA, HQ), const),
                  pl.BlockSpec((1, MLA_KV_LORA), const),
                  pl.BlockSpec((MLA_KV_LORA, MLA_HEADS * MLA_NOPE), const),
                  pl.BlockSpec((MLA_KV_LORA, MLA_DIM), const)],
        out_specs=(pl.BlockSpec((tm, HQ), row), pl.BlockSpec((tm, HQ), row),
                   pl.BlockSpec((tm, MLA_DIM), row)),
        compiler_params=_cparams("parallel"),
        name="mla_proj",
    )(proj2d, proj2d, proj2d, cos, sin, q_norm_w.reshape(1, -1), wq, kv_norm_w.reshape(1, -1), wk, wv)


def _attn_kernel(q_ref, k_ref, v_ref, o_ref, m_ref, l_ref, acc_ref, *, tq, tk):
    qi, ki = pl.program_id(2), pl.program_id(3)

    @pl.when(ki == 0)
    def _():
        m_ref[...] = jnp.full_like(m_ref, -1e30)
        l_ref[...] = jnp.zeros_like(l_ref)
        acc_ref[...] = jnp.zeros_like(acc_ref)

    @pl.when(ki * tk <= qi * tq + tq - 1)
    def _():
        s = lax.dot_general(q_ref[...], k_ref[...], (((1,), (1,)), ((), ())), preferred_element_type=F32)
        row = qi * tq + lax.broadcasted_iota(jnp.int32, s.shape, 0)
        col = ki * tk + lax.broadcasted_iota(jnp.int32, s.shape, 1)
        s = jnp.where(row >= col, s, -1e30)
        m_prev = m_ref[...]
        m_new = jnp.maximum(m_prev, jnp.max(s, axis=-1, keepdims=True))
        alpha = jnp.exp(m_prev - m_new)
        p = jnp.exp(s - m_new)
        l_ref[...] = alpha * l_ref[...] + jnp.sum(p, axis=-1, keepdims=True)
        acc_ref[...] = alpha * acc_ref[...] + jnp.dot(p.astype(BF16), v_ref[...], preferred_element_type=F32)
        m_ref[...] = m_new

    @pl.when(ki == pl.num_programs(3) - 1)
    def _():
        o_ref[...] = acc_ref[...] / l_ref[...]


def _attention(q, k, v):
    B, S, _ = q.shape
    tq = tk = min(S, 512)
    last = lambda qi: (qi * tq + tq - 1) // tk
    return pl.pallas_call(
        functools.partial(_attn_kernel, tq=tq, tk=tk),
        out_shape=jax.ShapeDtypeStruct((B, S, MLA_DIM), F32),
        grid=(B, MLA_HEADS, S // tq, S // tk),
        in_specs=[pl.BlockSpec((None, tq, QK_W), lambda b, h, qi, ki: (b, qi, h)),
                  pl.BlockSpec((None, tk, QK_W), lambda b, h, qi, ki: (b, jnp.minimum(ki, last(qi)), h)),
                  pl.BlockSpec((None, tk, MLA_V), lambda b, h, qi, ki: (b, jnp.minimum(ki, last(qi)), h))],
        out_specs=pl.BlockSpec((None, tq, MLA_V), lambda b, h, qi, ki: (b, qi, h)),
        scratch_shapes=[pltpu.VMEM((tq, 1), F32), pltpu.VMEM((tq, 1), F32), pltpu.VMEM((tq, MLA_V), F32)],
        compiler_params=_cparams("parallel", "parallel", "parallel", "arbitrary"),
        name="mla_attention",
    )(q, k, v)


def _merge_kernel(ya_ref, yb_ref, yc_ref, ga_ref, gb_ref, gc_ref, x_ref, g1_ref,
                  pa_ref, pb_ref, pc_ref, wo_ref, o_ref):
    m = (_sigmoid(ga_ref[...]) * _dot(ya_ref[...], pa_ref[...])
         + _sigmoid(gb_ref[...]) * _dot(yb_ref[...], pb_ref[...])
         + _sigmoid(gc_ref[...]) * _dot(yc_ref[...], pc_ref[...]))
    o_ref[...] = x_ref[...] + g1_ref[...] * _dot(m, wo_ref[...])


def _merge(ya, yb, yc, proj, x, g1, pa, pb, pc, wo):
    B, S, D = x.shape
    tm = min(S, 512)
    W = ya.shape[-1]
    row = lambda b, i: (b, i, 0)
    const = lambda b, i: (0, 0)
    gate = lambda n: pl.BlockSpec((None, tm, D), lambda b, i: (b, i, COL_GATES // D + n))
    return pl.pallas_call(
        _merge_kernel,
        out_shape=jax.ShapeDtypeStruct((B, S, D), F32),
        grid=(B, S // tm),
        in_specs=[pl.BlockSpec((None, tm, W), row), pl.BlockSpec((None, tm, W), row),
                  pl.BlockSpec((None, tm, W), row), gate(0), gate(1), gate(2),
                  pl.BlockSpec((None, tm, D), row),
                  pl.BlockSpec((None, 1, D), lambda b, i: (b, 0, 0)),
                  pl.BlockSpec((W, D), const), pl.BlockSpec((W, D), const), pl.BlockSpec((W, D), const),
                  pl.BlockSpec((D, D), const)],
        out_specs=pl.BlockSpec((None, tm, D), row),
        compiler_params=_cparams("parallel", "parallel"),
        name="branch_merge",
    )(ya, yb, yc, proj, proj, proj, x, g1, pa, pb, pc, wo)


def _ffn_kernel(x_ref, sc_ref, sh_ref, g2_ref, wg_ref, wu_ref, wd_ref, o_ref, h_ref, acc_ref):
    f = pl.program_id(2)

    @pl.when(f == 0)
    def _():
        h = _rms(x_ref[...]) * (1.0 + sc_ref[...]) + sh_ref[...]
        h_ref[...] = h.astype(BF16)
        acc_ref[...] = jnp.zeros_like(acc_ref)

    h = h_ref[...]
    a = _silu(jnp.dot(h, wg_ref[...], preferred_element_type=F32)) \
        * jnp.dot(h, wu_ref[...], preferred_element_type=F32)
    acc_ref[...] += jnp.dot(a.astype(BF16), wd_ref[...], preferred_element_type=F32)

    @pl.when(f == pl.num_programs(2) - 1)
    def _():
        o_ref[...] = x_ref[...] + g2_ref[...] * acc_ref[...]


def _dense_ffn(x, sc, sh, g2, wg, wu, wd):
    B, S, D = x.shape
    F = wg.shape[1]
    tm = min(S, 1024)
    tf = 256
    row = lambda b, i, f: (b, i, 0)
    vec = lambda b, i, f: (b, 0, 0)
    return pl.pallas_call(
        _ffn_kernel,
        out_shape=jax.ShapeDtypeStruct((B, S, D), F32),
        grid=(B, S // tm, F // tf),
        in_specs=[pl.BlockSpec((None, tm, D), row),
                  pl.BlockSpec((None, 1, D), vec), pl.BlockSpec((None, 1, D), vec),
                  pl.BlockSpec((None, 1, D), vec),
                  pl.BlockSpec((D, tf), lambda b, i, f: (0, f)),
                  pl.BlockSpec((D, tf), lambda b, i, f: (0, f)),
                  pl.BlockSpec((tf, D), lambda b, i, f: (f, 0))],
        out_specs=pl.BlockSpec((None, tm, D), row),
        scratch_shapes=[pltpu.VMEM((tm, D), BF16), pltpu.VMEM((tm, D), F32)],
        compiler_params=_cparams("parallel", "parallel", "arbitrary"),
        name="dense_ffn",
    )(x, sc, sh, g2, wg, wu, wd)


def _moe_pre_kernel(x_ref, sc_ref, sh_ref, r_ref, h_ref, top_ref):
    h = _rms(x_ref[...]) * (1.0 + sc_ref[...]) + sh_ref[...]
    h_ref[...] = h
    logits = _dot_hi(h, r_ref[...])
    lane = lax.broadcasted_iota(jnp.int32, logits.shape, 1)
    lane_f = lane.astype(F32)
    logits = jnp.where(lane < N_EXPERTS, logits, -jnp.inf)
    m1 = jnp.max(logits, axis=-1, keepdims=True)
    i1 = jnp.min(jnp.where(logits == m1, lane_f, float(LANES)), axis=-1, keepdims=True)
    rest = jnp.where(lane_f == i1, -jnp.inf, logits)
    m2 = jnp.max(rest, axis=-1, keepdims=True)
    i2 = jnp.min(jnp.where(rest == m2, lane_f, float(LANES)), axis=-1, keepdims=True)
    e2 = jnp.exp(m2 - m1)
    w1 = 1.0 / (1.0 + e2)
    w2 = e2 / (1.0 + e2)
    top_ref[...] = jnp.where(lane == 0, i1, jnp.where(lane == 1, i2,
                             jnp.where(lane == 2, w1, jnp.where(lane == 3, w2, 0.0))))


def _moe_pre(x, sc, sh, router_pad):
    B, S, D = x.shape
    tm = min(S, 512)
    row = lambda b, i: (b, i, 0)
    vec = lambda b, i: (b, 0, 0)
    return pl.pallas_call(
        _moe_pre_kernel,
        out_shape=(jax.ShapeDtypeStruct((B, S, D), F32), jax.ShapeDtypeStruct((B, S, LANES), F32)),
        grid=(B, S // tm),
        in_specs=[pl.BlockSpec((None, tm, D), row), pl.BlockSpec((None, 1, D), vec),
                  pl.BlockSpec((None, 1, D), vec), pl.BlockSpec((D, LANES), lambda b, i: (0, 0))],
        out_specs=(pl.BlockSpec((None, tm, D), row), pl.BlockSpec((None, tm, LANES), row)),
        compiler_params=_cparams("parallel", "parallel"),
        name="moe_route",
    )(x, sc, sh, router_pad)


def _moe_gather_kernel(tok_ref, h_ref, o_ref, buf_ref, sem):
    def issue(r, carry):
        pltpu.make_async_copy(h_ref.at[pl.ds(tok_ref[0, 0, r], 1)], buf_ref.at[pl.ds(r, 1)], sem).start()
        return carry

    lax.fori_loop(0, MOE_BLOCK, issue, 0)
    pltpu.make_async_copy(h_ref.at[pl.ds(0, MOE_BLOCK)], buf_ref, sem).wait()
    o_ref[...] = buf_ref[...].astype(BF16)


def _moe_gather(h2d, slot_tok):
    T, D = h2d.shape
    n_blocks = slot_tok.shape[0] // MOE_BLOCK
    return pl.pallas_call(
        _moe_gather_kernel,
        out_shape=jax.ShapeDtypeStruct((n_blocks * MOE_BLOCK, D), BF16),
        grid=(n_blocks,),
        in_specs=[pl.BlockSpec((1, 1, MOE_BLOCK), lambda i: (i, 0, 0), memory_space=pltpu.SMEM),
                  pl.BlockSpec(memory_space=pl.ANY)],
        out_specs=pl.BlockSpec((MOE_BLOCK, D), lambda i: (i, 0)),
        scratch_shapes=[pltpu.VMEM((MOE_BLOCK, D), F32), pltpu.SemaphoreType.DMA],
        compiler_params=_cparams("arbitrary"),
        name="moe_gather",
    )(slot_tok.reshape(n_blocks, 1, MOE_BLOCK), h2d)


def _moe_up_kernel(be_ref, x_ref, wg_ref, wu_ref, o_ref, wg_bf, wu_bf):
    b = pl.program_id(1)
    prev = be_ref[jnp.maximum(b - 1, 0)]

    @pl.when((b == 0) | (be_ref[b] != prev))
    def _():
        wg_bf[...] = wg_ref[...].astype(BF16)
        wu_bf[...] = wu_ref[...].astype(BF16)

    x = x_ref[...]
    a = _silu(jnp.dot(x, wg_bf[...], preferred_element_type=F32)) \
        * jnp.dot(x, wu_bf[...], preferred_element_type=F32)
    o_ref[...] = a.astype(BF16)


def _moe_up(x_slots, block_e, w_gate, w_up, layer):
    slots, D = x_slots.shape
    F = w_gate.shape[-1]
    tf = 896
    n_blocks = slots // MOE_BLOCK
    wspec = pl.BlockSpec((None, None, D, tf), lambda f, b, be: (layer, be[b], 0, f))
    return pl.pallas_call(
        _moe_up_kernel,
        out_shape=jax.ShapeDtypeStruct((slots, F), BF16),
        grid_spec=pltpu.PrefetchScalarGridSpec(
            num_scalar_prefetch=1,
            grid=(F // tf, n_blocks),
            in_specs=[pl.BlockSpec((MOE_BLOCK, D), lambda f, b, be: (b, 0)), wspec, wspec],
            out_specs=pl.BlockSpec((MOE_BLOCK, tf), lambda f, b, be: (b, f)),
            scratch_shapes=[pltpu.VMEM((D, tf), BF16), pltpu.VMEM((D, tf), BF16)]),
        compiler_params=_cparams("arbitrary", "arbitrary"),
        name="moe_gate_up",
    )(block_e, x_slots, w_gate, w_up)


def _moe_down_kernel(be_ref, a_ref, wd_ref, o_ref, wd_bf):
    b = pl.program_id(0)
    prev = be_ref[jnp.maximum(b - 1, 0)]

    @pl.when((b == 0) | (be_ref[b] != prev))
    def _():
        wd_bf[...] = wd_ref[...].astype(BF16)

    o_ref[...] = jnp.dot(a_ref[...], wd_bf[...], preferred_element_type=F32)


def _moe_down(a_slots, block_e, w_down, layer):
    slots, F = a_slots.shape
    D = w_down.shape[-1]
    n_blocks = slots // MOE_BLOCK
    return pl.pallas_call(
        _moe_down_kernel,
        out_shape=jax.ShapeDtypeStruct((slots, D), F32),
        grid_spec=pltpu.PrefetchScalarGridSpec(
            num_scalar_prefetch=1,
            grid=(n_blocks,),
            in_specs=[pl.BlockSpec((MOE_BLOCK, F), lambda b, be: (b, 0)),
                      pl.BlockSpec((None, None, F, D), lambda b, be: (layer, be[b], 0, 0))],
            out_specs=pl.BlockSpec((MOE_BLOCK, D), lambda b, be: (b, 0)),
            scratch_shapes=[pltpu.VMEM((F, D), BF16)]),
        compiler_params=_cparams("arbitrary"),
        name="moe_down",
    )(block_e, a_slots, w_down)


def _moe_combine_kernel(dest_ref, x_ref, g2_ref, top_ref, y_ref, o_ref, buf_ref, sem, *, tm):
    def issue(r, carry):
        for k in range(TOP_K):
            pltpu.make_async_copy(y_ref.at[pl.ds(dest_ref[0, 0, TOP_K * r + k], 1)],
                                  buf_ref.at[k, pl.ds(r, 1)], sem).start()
        return carry

    lax.fori_loop(0, tm, issue, 0)
    for k in range(TOP_K):
        pltpu.make_async_copy(y_ref.at[pl.ds(0, tm)], buf_ref.at[k], sem).wait()
    top = top_ref[...]
    f = top[:, 2:3] * buf_ref[0] + top[:, 3:4] * buf_ref[1]
    o_ref[...] = x_ref[...] + g2_ref[...] * f


def _moe_combine(x, g2, top, dest, y_slots):
    B, S, D = x.shape
    tm = min(S, 256)
    nb = S // tm
    row = lambda b, i: (b, i, 0)
    return pl.pallas_call(
        functools.partial(_moe_combine_kernel, tm=tm),
        out_shape=jax.ShapeDtypeStruct((B, S, D), F32),
        grid=(B, nb),
        in_specs=[pl.BlockSpec((1, 1, TOP_K * tm), lambda b, i: (b * nb + i, 0, 0), memory_space=pltpu.SMEM),
                  pl.BlockSpec((None, tm, D), row),
                  pl.BlockSpec((None, 1, D), lambda b, i: (b, 0, 0)),
                  pl.BlockSpec((None, tm, LANES), row),
                  pl.BlockSpec(memory_space=pl.ANY)],
        out_specs=pl.BlockSpec((None, tm, D), row),
        scratch_shapes=[pltpu.VMEM((TOP_K, tm, D), F32), pltpu.SemaphoreType.DMA],
        compiler_params=_cparams("arbitrary", "arbitrary"),
        name="moe_combine",
    )(dest.reshape(B * nb, 1, TOP_K * tm), x, g2, top, y_slots)


def _moe_ffn(x, sc, sh, g2, router, w_gate, w_up, w_down, layer):
    B, S, D = x.shape
    T = B * S
    router_pad = jnp.zeros((D, LANES), F32).at[:, :N_EXPERTS].set(router)
    h, top = _moe_pre(x, sc, sh, router_pad)
    top2d = top.reshape(T, LANES)
    flat_e = top2d[:, :TOP_K].astype(jnp.int32).reshape(-1)
    n_assign = T * TOP_K
    n_blocks = (n_assign + N_EXPERTS * (MOE_BLOCK - 1)) // MOE_BLOCK
    slots = n_blocks * MOE_BLOCK
    onehot = (flat_e[:, None] == jnp.arange(N_EXPERTS, dtype=jnp.int32)[None, :]).astype(jnp.int32)
    csum = jnp.cumsum(onehot, axis=0)
    rank = jnp.sum(csum * onehot, axis=1) - 1
    counts = csum[-1]
    padded = (counts + MOE_BLOCK - 1) // MOE_BLOCK * MOE_BLOCK
    pad_end = jnp.cumsum(padded)
    start_padded = pad_end - padded
    dest = (start_padded[flat_e] + rank).astype(jnp.int32)
    flat_tok = jnp.repeat(jnp.arange(T, dtype=jnp.int32), TOP_K)
    slot_tok = jnp.zeros((slots,), jnp.int32).at[dest].set(flat_tok)
    block_e = jnp.minimum(jnp.searchsorted(pad_end, jnp.arange(n_blocks) * MOE_BLOCK, side='right'),
                          N_EXPERTS - 1).astype(jnp.int32)
    x_slots = _moe_gather(h.reshape(T, D), slot_tok)
    a_slots = _moe_up(x_slots, block_e, w_gate, w_up, layer)
    y_slots = _moe_down(a_slots, block_e, w_down, layer)
    return _moe_combine(x, g2, top, dest, y_slots)


def _final_norm_kernel(x_ref, w_ref, o_ref):
    o_ref[...] = _rms(x_ref[...]) * w_ref[...]


def _final_norm(x2d, w):
    T, D = x2d.shape
    tm = min(T, 1024)
    return pl.pallas_call(
        _final_norm_kernel,
        out_shape=jax.ShapeDtypeStruct((T, D), F32),
        grid=(T // tm,),
        in_specs=[pl.BlockSpec((tm, D), lambda i: (i, 0)), pl.BlockSpec((1, D), lambda i: (0, 0))],
        out_specs=pl.BlockSpec((tm, D), lambda i: (i, 0)),
        compiler_params=_cparams("parallel"),
        name="final_norm",
    )(x2d, w.reshape(1, D))


def _arrange_w_in(w):
    D = w.shape[0]
    o = np.cumsum([0, 3 * GDN_DIM, GDN_DIM, GDN_HEADS, GDN_HEADS, SSM_INNER, SSM_CONV_DIM, SSM_HEADS,
                   MLA_Q_LORA, MLA_KV_LORA, MLA_ROPE, 3 * w.shape[0]])
    seg = lambda i: w[:, int(o[i]):int(o[i + 1])]
    qkv, gz, ga, gb, sz, xbc, dt, cq, ckv, kr, gates = [seg(i) for i in range(11)]
    half = MLA_ROPE // 2
    small = jnp.concatenate([ga, gb, dt, jnp.zeros((D, LANES - 2 * GDN_HEADS - SSM_HEADS), w.dtype)], axis=1)
    kr_full = jnp.concatenate([kr, -kr[:, half:], kr[:, :half]], axis=1)
    out = jnp.concatenate([qkv, gz, sz, cq, xbc, gates, ckv, small, kr_full], axis=1)
    assert out.shape[1] == IN_COLS
    return out.astype(BF16)


def _arrange_w_uq(w):
    half = MLA_ROPE // 2
    parts = []
    for h in range(MLA_HEADS):
        c0 = h * (MLA_NOPE + MLA_ROPE)
        rope = w[:, c0 + MLA_NOPE:c0 + MLA_NOPE + MLA_ROPE]
        parts += [w[:, c0:c0 + MLA_NOPE], rope, -rope[:, half:], rope[:, :half]]
    return jnp.concatenate(parts, axis=1).astype(BF16)


def kernel(x, c, positions, w_ada, b_ada, w_in, gdn_conv_w, gdn_a_log, gdn_dt_bias, gdn_norm_w, ssm_conv_w, ssm_conv_b, ssm_a_log, ssm_dt_bias, ssm_d, ssm_norm_w, mla_q_norm_w, mla_w_uq, mla_kv_norm_w, mla_w_uk, mla_w_uv, w_branch_a, w_branch_b, w_branch_c, w_out, ffn_w_gate, ffn_w_up, ffn_w_down, moe_router, moe_w_gate, moe_w_up, moe_w_down, final_norm_w):
    B, S, D = x.shape
    T = B * S
    depth = w_ada.shape[0]
    c_pad = jnp.zeros((SUBLANES, D), F32).at[:B].set(c)
    mod = _modulation(c_pad, w_ada.astype(BF16), b_ada)[:, :B].reshape(depth, B, 6, 1, D)
    cos, sin = _rope_tables(positions)
    for l in range(depth):
        sh1, sc1, g1, sh2, sc2, g2 = [mod[l, :, n] for n in range(6)]
        proj = _in_proj(x, sc1, sh1, _arrange_w_in(w_in[l]))
        y_a = _gdn(proj, gdn_conv_w[l], gdn_a_log[l], gdn_dt_bias[l], gdn_norm_w[l])
        y_b = _ssd(proj, ssm_conv_w[l], ssm_conv_b[l], ssm_a_log[l], ssm_dt_bias[l], ssm_d[l], ssm_norm_w[l])
        q, k, v = _mla_proj(proj.reshape(T, IN_COLS), cos, sin, mla_q_norm_w[l], _arrange_w_uq(mla_w_uq[l]),
                            mla_kv_norm_w[l], mla_w_uk[l].astype(BF16), mla_w_uv[l].astype(BF16))
        y_c = _attention(q.reshape(B, S, -1), k.reshape(B, S, -1), v.reshape(B, S, -1))
        x = _merge(y_a, y_b, y_c, proj, x, g1, w_branch_a[l].astype(BF16), w_branch_b[l].astype(BF16),
                   w_branch_c[l].astype(BF16), w_out[l].astype(BF16))
        i = l // 2
        if l % 2 == 0:
            x = _dense_ffn(x, sc2, sh2, g2, ffn_w_gate[i].astype(BF16), ffn_w_up[i].astype(BF16),
                           ffn_w_down[i].astype(BF16))
        else:
            x = _moe_ffn(x, sc2, sh2, g2, moe_router[i], moe_w_gate, moe_w_up, moe_w_down, i)
    return _final_norm(x.reshape(T, D), final_norm_w).reshape(B, S, D)
```

```python
import functools
import math

import numpy as np
import jax
import jax.numpy as jnp
from jax import lax
from jax.experimental import pallas as pl
from jax.experimental.pallas import tpu as pltpu

F32 = jnp.float32
BF16 = jnp.bfloat16
HI = lax.Precision.HIGHEST

EPS = 1e-6
CHUNK = 64
CONV_K = 4
GDN_HEADS = 4
GDN_HEAD_DIM = 128
GDN_DIM = GDN_HEADS * GDN_HEAD_DIM
SSM_HEADS = 8
SSM_HEAD_DIM = 64
SSM_INNER = SSM_HEADS * SSM_HEAD_DIM
SSM_GROUPS = 2
SSM_STATE = 128
SSM_CONV_DIM = SSM_INNER + 2 * SSM_GROUPS * SSM_STATE
MLA_HEADS = 4
MLA_Q_LORA = 512
MLA_KV_LORA = 256
MLA_NOPE = 128
MLA_ROPE = 64
MLA_V = 128
MLA_DIM = MLA_HEADS * MLA_V
ROPE_THETA = 10000.0
N_EXPERTS = 8
TOP_K = 2
MOE_BLOCK = 256

LANES = 128
SUBLANES = 8
VMEM_LIMIT = 56 * 1024 * 1024

COL_QKV = 0
COL_GZ = 1536
COL_SZ = 2048
COL_CQ = 2560
COL_XBC = 3072
COL_GATES = 4096
COL_CKV = 7168
COL_SMALL = 7424
COL_KR = 7552
IN_COLS = 7680


def _cparams(*sem):
    return pltpu.CompilerParams(dimension_semantics=tuple(sem), vmem_limit_bytes=VMEM_LIMIT)


def _silu(x):
    return x * (1.0 / (1.0 + jnp.exp(-x)))


def _sigmoid(x):
    return 1.0 / (1.0 + jnp.exp(-x))


def _softplus(x):
    return jnp.maximum(x, 0.0) + jnp.log(1.0 + jnp.exp(-jnp.abs(x)))


def _rms(x):
    return x * lax.rsqrt(jnp.mean(x * x, axis=-1, keepdims=True) + EPS)


def _dot(a, b):
    return jnp.dot(a.astype(BF16), b.astype(BF16), preferred_element_type=F32)


def _dot_nt(a, b):
    return lax.dot_general(a.astype(BF16), b.astype(BF16), (((1,), (1,)), ((), ())),
                           preferred_element_type=F32)


def _dot_tn(a, b):
    return lax.dot_general(a.astype(BF16), b.astype(BF16), (((0,), (0,)), ((), ())),
                           preferred_element_type=F32)


def _dot_hi(a, b):
    return jnp.dot(a, b, precision=HI, preferred_element_type=F32)


def _mod_kernel(c_ref, w_ref, b_ref, o_ref):
    cond = _silu(c_ref[...])
    o_ref[...] = _dot(cond, w_ref[...]) + b_ref[...]


def _modulation(c_pad, w_ada, b_ada):
    L, D, N = w_ada.shape
    tn = 1536
    return pl.pallas_call(
        _mod_kernel,
        out_shape=jax.ShapeDtypeStruct((L, SUBLANES, N), F32),
        grid=(L, N // tn),
        in_specs=[pl.BlockSpec((SUBLANES, D), lambda l, j: (0, 0)),
                  pl.BlockSpec((None, D, tn), lambda l, j: (l, 0, j)),
                  pl.BlockSpec((None, 1, tn), lambda l, j: (l, 0, j))],
        out_specs=pl.BlockSpec((None, SUBLANES, tn), lambda l, j: (l, 0, j)),
        compiler_params=_cparams("parallel", "parallel"),
        name="adaln_mod",
    )(c_pad, w_ada, b_ada.reshape(L, 1, N))


def _rope_table_kernel(pos_ref, freq_ref, cos_ref, sin_ref):
    ang = pos_ref[...].astype(F32) * freq_ref[...]
    keep = lax.broadcasted_iota(jnp.int32, ang.shape, 1) < MLA_ROPE
    cos_ref[...] = jnp.where(keep, jnp.cos(ang), 0.0)
    sin_ref[...] = jnp.where(keep, jnp.sin(ang), 0.0)


def _rope_tables(positions):
    T = positions.size
    tm = min(T, 1024)
    inv_freq = ROPE_THETA ** (-jnp.arange(0, MLA_ROPE, 2, dtype=F32) / MLA_ROPE)
    freq = jnp.concatenate([inv_freq, inv_freq, jnp.zeros((LANES - MLA_ROPE,), F32)]).reshape(1, LANES)
    return pl.pallas_call(
        _rope_table_kernel,
        out_shape=(jax.ShapeDtypeStruct((T, LANES), F32), jax.ShapeDtypeStruct((T, LANES), F32)),
        grid=(T // tm,),
        in_specs=[pl.BlockSpec((tm, 1), lambda i: (i, 0)),
                  pl.BlockSpec((1, LANES), lambda i: (0, 0))],
        out_specs=(pl.BlockSpec((tm, LANES), lambda i: (i, 0)),
                   pl.BlockSpec((tm, LANES), lambda i: (i, 0))),
        compiler_params=_cparams("parallel"),
        name="rope_tables",
    )(positions.reshape(T, 1), freq)


def _in_proj_kernel(x_ref, sc_ref, sh_ref, w_ref, o_ref, h_ref):
    @pl.when(pl.program_id(2) == 0)
    def _():
        h = _rms(x_ref[...]) * (1.0 + sc_ref[...]) + sh_ref[...]
        h_ref[...] = h.astype(BF16)

    o_ref[...] = jnp.dot(h_ref[...], w_ref[...], preferred_element_type=F32)


def _in_proj(x, sc, sh, w):
    B, S, D = x.shape
    N = w.shape[1]
    tm = min(S, 1024)
    tn = 1536
    return pl.pallas_call(
        _in_proj_kernel,
        out_shape=jax.ShapeDtypeStruct((B, S, N), F32),
        grid=(B, S // tm, N // tn),
        in_specs=[pl.BlockSpec((None, tm, D), lambda b, i, j: (b, i, 0)),
                  pl.BlockSpec((None, 1, D), lambda b, i, j: (b, 0, 0)),
                  pl.BlockSpec((None, 1, D), lambda b, i, j: (b, 0, 0)),
                  pl.BlockSpec((D, tn), lambda b, i, j: (0, j))],
        out_specs=pl.BlockSpec((None, tm, tn), lambda b, i, j: (b, i, j)),
        scratch_shapes=[pltpu.VMEM((tm, D), BF16)],
        compiler_params=_cparams("parallel", "parallel", "arbitrary"),
        name="in_proj",
    )(x, sc, sh, w)


def _causal_conv(x, tail_ref, w):
    rows = x.shape[0]
    xe = jnp.concatenate([tail_ref[...], x], axis=0)
    tail_ref[...] = x[rows - SUBLANES:, :]
    y = xe[SUBLANES:, :] * w[CONV_K - 1:CONV_K, :]
    for j in range(CONV_K - 1):
        shift = CONV_K - 1 - j
        y = y + pltpu.roll(xe, shift, axis=0)[SUBLANES:, :] * w[j:j + 1, :]
    return y


def _chunk_cumsum(v, chunk=CHUNK):
    pos = lax.broadcasted_iota(jnp.int32, v.shape, 0) % chunk
    step = 1
    while step < chunk:
        v = v + jnp.where(pos >= step, pltpu.roll(v, step, axis=0), 0.0)
        step *= 2
    return v


def _col_to_row(col_b, eye):
    return jnp.sum(jnp.where(eye, col_b, 0.0), axis=0, keepdims=True)


GDN_CHUNK = 128
GDN_PASSES = 3


def _split(a):
    hi = a.astype(BF16)
    return hi, (a - hi.astype(F32)).astype(BF16)


def _mm_split(a, b):
    d = lambda x, y: jnp.dot(x, y, preferred_element_type=F32)
    if GDN_PASSES == 1:
        return d(a[0], b[0])
    return d(a[0], b[0]) + (d(a[0], b[1]) + d(a[1], b[0]))


def _gdn_kernel(qkv_ref, z_ref, sm_ref, cw_ref, par_ref, nw_ref, o_ref,
                tail_ref, state_ref, *, rows):
    C, H, Dh = GDN_CHUNK, GDN_HEADS, GDN_HEAD_DIM
    n_ck = rows // C

    @pl.when(pl.program_id(1) == 0)
    def _():
        tail_ref[...] = jnp.zeros_like(tail_ref)
        state_ref[...] = jnp.zeros_like(state_ref)

    y = _silu(_causal_conv(qkv_ref[...], tail_ref, cw_ref[...]))
    sm = sm_ref[...]
    neg_a = par_ref[0:1, :]
    dt_b = par_ref[1:2, :]
    g_all = _chunk_cumsum(neg_a * _softplus(sm + dt_b), C)
    beta_all = _sigmoid(sm)
    z = z_ref[...]
    nw = nw_ref[...]

    ri = lax.broadcasted_iota(jnp.int32, (C, C), 0)
    ci = lax.broadcasted_iota(jnp.int32, (C, C), 1)
    eye = ri == ci
    incl = ri >= ci
    strict = ri > ci
    eye_f = jnp.where(eye, 1.0, 0.0)

    qn, kn, vn = [], [], []
    for h in range(H):
        q_h = y[:, h * Dh:(h + 1) * Dh]
        k_h = y[:, GDN_DIM + h * Dh:GDN_DIM + (h + 1) * Dh]
        qn.append(q_h * lax.rsqrt(jnp.sum(q_h * q_h, axis=-1, keepdims=True) + EPS) * (Dh ** -0.5))
        kn.append(k_h * lax.rsqrt(jnp.sum(k_h * k_h, axis=-1, keepdims=True) + EPS))
        vn.append(y[:, 2 * GDN_DIM + h * Dh:2 * GDN_DIM + (h + 1) * Dh])

    items = [(c, h) for c in range(n_ck) for h in range(H)]
    p, rhs, qk, q_dec, k_dec, g_end = {}, {}, {}, {}, {}, {}
    for c, h in items:
        r0 = c * C
        q_c, k_c, v_c = qn[h][r0:r0 + C], kn[h][r0:r0 + C], vn[h][r0:r0 + C]
        g_col = g_all[r0:r0 + C, h:h + 1]
        beta = beta_all[r0:r0 + C, H + h:H + h + 1]
        g_b = jnp.broadcast_to(g_col, (C, C))
        g_last = g_col[C - 1:C, :]
        decay = jnp.exp(jnp.where(incl, g_b - _col_to_row(g_b, eye), -1e30))
        eg = jnp.exp(g_col)
        p[c, h] = jnp.where(strict, -(beta * _dot_nt(k_c, k_c) * decay), 0.0)
        qk[c, h] = (_dot_nt(q_c, k_c) * decay).astype(BF16)
        rhs[c, h] = _split(jnp.concatenate([v_c * beta, k_c * (beta * eg)], axis=-1))
        q_dec[c, h] = (q_c * eg).astype(BF16)
        k_dec[c, h] = (k_c * jnp.exp(g_last - g_col)).astype(BF16)
        g_end[c, h] = jnp.exp(g_last)

    inv = {it: eye_f + jnp.where((ri // 2 == ci // 2) & strict, p[it], 0.0) for it in items}
    s = 2
    while s < C:
        level = ((ri // s) % 2 == 1) & ((ci // s) % 2 == 0) & (ri // (2 * s) == ci // (2 * s))
        t_split = {it: _split(inv[it]) for it in items}
        x = {it: _mm_split(_split(jnp.where(level, p[it], 0.0)), t_split[it]) for it in items}
        inv = {it: inv[it] + _mm_split(t_split[it], _split(x[it])) for it in items}
        s *= 2
    sol = {it: _mm_split(_split(inv[it]), rhs[it]) for it in items}

    states = [state_ref[h] for h in range(H)]
    for c in range(n_ck):
        r0 = c * C
        s_bf = [s.astype(BF16) for s in states]
        v_new = [sol[c, h][:, :Dh] - _dot(sol[c, h][:, Dh:], s_bf[h]) for h in range(H)]
        v_bf = [v.astype(BF16) for v in v_new]
        outs = [jnp.dot(q_dec[c, h], s_bf[h], preferred_element_type=F32)
                + jnp.dot(qk[c, h], v_bf[h], preferred_element_type=F32) for h in range(H)]
        states = [states[h] * g_end[c, h]
                  + lax.dot_general(k_dec[c, h], v_bf[h], (((0,), (0,)), ((), ())), preferred_element_type=F32)
                  for h in range(H)]
        for h in range(H):
            z_c = z[r0:r0 + C, h * Dh:(h + 1) * Dh]
            o_ref[r0:r0 + C, h * Dh:(h + 1) * Dh] = _rms(outs[h]) * nw * _silu(z_c)
    for h in range(H):
        state_ref[h] = states[h]


def _gdn(proj, conv_w, a_log, dt_bias, norm_w):
    B, S, _ = proj.shape
    rows = min(S, 512)
    par = jnp.zeros((SUBLANES, LANES), F32)
    par = par.at[0, :GDN_HEADS].set(-jnp.exp(a_log)).at[1, :GDN_HEADS].set(dt_bias)
    w3 = 3 * GDN_DIM
    return pl.pallas_call(
        functools.partial(_gdn_kernel, rows=rows),
        out_shape=jax.ShapeDtypeStruct((B, S, GDN_DIM), F32),
        grid=(B, S // rows),
        in_specs=[pl.BlockSpec((None, rows, w3), lambda b, i: (b, i, COL_QKV // w3)),
                  pl.BlockSpec((None, rows, GDN_DIM), lambda b, i: (b, i, COL_GZ // GDN_DIM)),
                  pl.BlockSpec((None, rows, LANES), lambda b, i: (b, i, COL_SMALL // LANES)),
                  pl.BlockSpec((CONV_K, w3), lambda b, i: (0, 0)),
                  pl.BlockSpec((SUBLANES, LANES), lambda b, i: (0, 0)),
                  pl.BlockSpec((1, GDN_HEAD_DIM), lambda b, i: (0, 0))],
        out_specs=pl.BlockSpec((None, rows, GDN_DIM), lambda b, i: (b, i, 0)),
        scratch_shapes=[pltpu.VMEM((SUBLANES, w3), F32),
                        pltpu.VMEM((GDN_HEADS, GDN_HEAD_DIM, GDN_HEAD_DIM), F32)],
        compiler_params=_cparams("parallel", "arbitrary"),
        name="gated_delta_net",
    )(proj, proj, proj, conv_w, par, norm_w.reshape(1, GDN_HEAD_DIM))


SSD_CHUNK = 128


def _ssd_kernel(xbc_ref, z_ref, sm_ref, cw_ref, cb_ref, par_ref, nw_ref, o_ref,
                tail_ref, state_ref, *, rows):
    C, H, P, G, N = SSD_CHUNK, SSM_HEADS, SSM_HEAD_DIM, SSM_GROUPS, SSM_STATE
    E = H // G
    GW = E * P

    @pl.when(pl.program_id(1) == 0)
    def _():
        tail_ref[...] = jnp.zeros_like(tail_ref)
        state_ref[...] = jnp.zeros_like(state_ref)

    y = _silu(_causal_conv(xbc_ref[...], tail_ref, cw_ref[...]) + cb_ref[...])
    sm = sm_ref[...]
    neg_a = par_ref[0:1, :]
    dt_b = par_ref[1:2, :]
    d_skip = par_ref[2:3, :]
    dt_all = _softplus(sm + dt_b)
    acs_all = _chunk_cumsum(dt_all * neg_a, C)
    z = z_ref[...]
    nw = nw_ref[...]

    ri = lax.broadcasted_iota(jnp.int32, (C, C), 0)
    ci = lax.broadcasted_iota(jnp.int32, (C, C), 1)
    eye = ri == ci
    incl = ri >= ci

    items = [(c, g) for c in range(rows // C) for g in range(G)]
    cm, y_local, st_add, e_in, e_out = {}, {}, {}, {}, {}
    for c, g in items:
        r0 = c * C
        bm_c = y[r0:r0 + C, SSM_INNER + g * N:SSM_INNER + (g + 1) * N]
        cm_c = y[r0:r0 + C, SSM_INNER + G * N + g * N:SSM_INNER + G * N + (g + 1) * N]
        x_c = y[r0:r0 + C, g * GW:(g + 1) * GW]
        cb = _dot_nt(cm_c, bm_c)
        xs_parts, ea_parts, ydiag_parts, dx_parts, last_parts = [], [], [], [], []
        for e in range(E):
            lane = SUBLANES + g * E + e
            a_col = acs_all[r0:r0 + C, lane:lane + 1]
            dt_col = dt_all[r0:r0 + C, lane:lane + 1]
            a_b = jnp.broadcast_to(a_col, (C, C))
            a_last = a_col[C - 1:C, :]
            L = jnp.exp(jnp.where(incl, a_b - _col_to_row(a_b, eye), -1e30))
            x_e = x_c[:, e * P:(e + 1) * P]
            xd = x_e * dt_col
            ydiag_parts.append(_dot(cb * L, xd))
            xs_parts.append(xd * jnp.exp(a_last - a_col))
            ea_parts.append(jnp.broadcast_to(jnp.exp(a_col), (C, P)))
            last_parts.append(jnp.broadcast_to(jnp.exp(a_last), (1, P)))
            dx_parts.append(x_e * d_skip[:, lane:lane + 1])
        cm[c, g] = cm_c.astype(BF16)
        y_local[c, g] = jnp.concatenate(ydiag_parts, axis=-1) + jnp.concatenate(dx_parts, axis=-1)
        st_add[c, g] = _dot_tn(bm_c, jnp.concatenate(xs_parts, axis=-1))
        e_in[c, g] = jnp.concatenate(ea_parts, axis=-1)
        e_out[c, g] = jnp.concatenate(last_parts, axis=-1)

    states = [state_ref[g] for g in range(G)]
    for c in range(rows // C):
        r0 = c * C
        for g in range(G):
            y_off = jnp.dot(cm[c, g], states[g].astype(BF16), preferred_element_type=F32)
            y_c = (y_local[c, g] + y_off * e_in[c, g]) * _silu(z[r0:r0 + C, g * GW:(g + 1) * GW])
            o_ref[r0:r0 + C, g * GW:(g + 1) * GW] = _rms(y_c) * nw[:, g * GW:(g + 1) * GW]
            states[g] = states[g] * e_out[c, g] + st_add[c, g]
    for g in range(G):
        state_ref[g] = states[g]


def _ssd(proj, conv_w, conv_b, a_log, dt_bias, d_skip, norm_w):
    B, S, _ = proj.shape
    rows = min(S, 512)
    par = jnp.zeros((SUBLANES, LANES), F32)
    sl = slice(SUBLANES, SUBLANES + SSM_HEADS)
    par = par.at[0, sl].set(-jnp.exp(a_log)).at[1, sl].set(dt_bias).at[2, sl].set(d_skip)
    return pl.pallas_call(
        functools.partial(_ssd_kernel, rows=rows),
        out_shape=jax.ShapeDtypeStruct((B, S, SSM_INNER), F32),
        grid=(B, S // rows),
        in_specs=[pl.BlockSpec((None, rows, SSM_CONV_DIM), lambda b, i: (b, i, COL_XBC // SSM_CONV_DIM)),
                  pl.BlockSpec((None, rows, SSM_INNER), lambda b, i: (b, i, COL_SZ // SSM_INNER)),
                  pl.BlockSpec((None, rows, LANES), lambda b, i: (b, i, COL_SMALL // LANES)),
                  pl.BlockSpec((CONV_K, SSM_CONV_DIM), lambda b, i: (0, 0)),
                  pl.BlockSpec((1, SSM_CONV_DIM), lambda b, i: (0, 0)),
                  pl.BlockSpec((SUBLANES, LANES), lambda b, i: (0, 0)),
                  pl.BlockSpec((1, SSM_INNER), lambda b, i: (0, 0))],
        out_specs=pl.BlockSpec((None, rows, SSM_INNER), lambda b, i: (b, i, 0)),
        scratch_shapes=[pltpu.VMEM((SUBLANES, SSM_CONV_DIM), F32),
                        pltpu.VMEM((SSM_GROUPS, SSM_STATE, SSM_INNER // SSM_GROUPS), F32)],
        compiler_params=_cparams("parallel", "arbitrary"),
        name="mamba2_ssd",
    )(proj, proj, proj, conv_w, conv_b.reshape(1, -1), par, norm_w.reshape(1, -1))


QK_W = 2 * LANES


def _rope(x, cos, sin):
    return x * cos + pltpu.roll(x, MLA_ROPE, axis=1) * sin


def _mla_proj_kernel(cq_ref, ckv_ref, kr_ref, cos_ref, sin_ref, qw_ref, wq_ref, kvw_ref, wk_ref, wv_ref,
                     q_ref, k_ref, v_ref):
    scale = (MLA_NOPE + MLA_ROPE) ** -0.5
    cos, sin = cos_ref[...], sin_ref[...]
    qc = _rms(cq_ref[...]) * qw_ref[...]
    qf = _dot(qc, wq_ref[...]) * scale
    ckv = (_rms(ckv_ref[...]) * kvw_ref[...]).astype(BF16)
    kn = jnp.dot(ckv, wk_ref[...], preferred_element_type=F32)
    v_ref[...] = jnp.dot(ckv, wv_ref[...], preferred_element_type=F32).astype(BF16)
    kr = _rope(kr_ref[...], cos, sin).astype(BF16)
    for h in range(MLA_HEADS):
        c0 = h * QK_W
        q_ref[:, c0:c0 + LANES] = qf[:, c0:c0 + LANES].astype(BF16)
        q_ref[:, c0 + LANES:c0 + QK_W] = _rope(qf[:, c0 + LANES:c0 + QK_W], cos, sin).astype(BF16)
        k_ref[:, c0:c0 + LANES] = kn[:, h * MLA_NOPE:(h + 1) * MLA_NOPE].astype(BF16)
        k_ref[:, c0 + LANES:c0 + QK_W] = kr


def _mla_proj(proj2d, cos, sin, q_norm_w, wq, kv_norm_w, wk, wv):
    T = proj2d.shape[0]
    tm = min(T, 512)
    HQ = MLA_HEADS * QK_W
    row = lambda i: (i, 0)
    const = lambda i: (0, 0)
    return pl.pallas_call(
        _mla_proj_kernel,
        out_shape=(jax.ShapeDtypeStruct((T, HQ), BF16), jax.ShapeDtypeStruct((T, HQ), BF16),
                   jax.ShapeDtypeStruct((T, MLA_DIM), BF16)),
        grid=(T // tm,),
        in_specs=[pl.BlockSpec((tm, MLA_Q_LORA), lambda i: (i, COL_CQ // MLA_Q_LORA)),
                  pl.BlockSpec((tm, MLA_KV_LORA), lambda i: (i, COL_CKV // MLA_KV_LORA)),
                  pl.BlockSpec((tm, LANES), lambda i: (i, COL_KR // LANES)),
                  pl.BlockSpec((tm, LANES), row),
                  pl.BlockSpec((tm, LANES), row),
                  pl.BlockSpec((1, MLA_Q_LORA), const),
                  pl.BlockSpec((MLA_Q_LORA, HQ), const),
                  pl.BlockSpec((1, MLA_KV_LORA), const),
                  pl.BlockSpec((MLA_KV_LORA, MLA_HEADS * MLA_NOPE), const),
                  pl.BlockSpec((MLA_KV_LORA, MLA_DIM), const)],
        out_specs=(pl.BlockSpec((tm, HQ), row), pl.BlockSpec((tm, HQ), row),
                   pl.BlockSpec((tm, MLA_DIM), row)),
        compiler_params=_cparams("parallel"),
        name="mla_proj",
    )(proj2d, proj2d, proj2d, cos, sin, q_norm_w.reshape(1, -1), wq, kv_norm_w.reshape(1, -1), wk, wv)


def _attn_kernel(q_ref, k_ref, v_ref, o_ref, m_ref, l_ref, acc_ref, *, t):
    qi = pl.program_id(1)
    m_ref[...] = jnp.full_like(m_ref, -1e30)
    l_ref[...] = jnp.zeros_like(l_ref)
    acc_ref[...] = jnp.zeros_like(acc_ref)

    def block(j, diagonal):
        k0 = pl.multiple_of(j * t, t)
        hs = range(MLA_HEADS)
        s = [lax.dot_general(q_ref[:, h * QK_W:(h + 1) * QK_W], k_ref[pl.ds(k0, t), h * QK_W:(h + 1) * QK_W],
                             (((1,), (1,)), ((), ())), preferred_element_type=F32) for h in hs]
        if diagonal:
            row = lax.broadcasted_iota(jnp.int32, s[0].shape, 0)
            col = lax.broadcasted_iota(jnp.int32, s[0].shape, 1)
            s = [jnp.where(row >= col, s[h], -1e30) for h in hs]
        cols = range(t // LANES)
        sc = [[s[h][:, c * LANES:(c + 1) * LANES] for c in cols] for h in hs]
        m_prev = [m_ref[h] for h in hs]
        m_new = [jnp.maximum(m_prev[h], jnp.max(functools.reduce(jnp.maximum, sc[h]), axis=-1, keepdims=True))
                 for h in hs]
        p = [[jnp.exp(sc[h][c] - m_new[h]) for c in cols] for h in hs]
        alpha = [jnp.exp(m_prev[h] - m_new[h]) for h in hs]
        pv = [jnp.dot(jnp.concatenate(p[h], axis=-1).astype(BF16), v_ref[pl.ds(k0, t), h * MLA_V:(h + 1) * MLA_V],
                      preferred_element_type=F32) for h in hs]
        for h in hs:
            l_ref[h] = alpha[h] * l_ref[h] + functools.reduce(jnp.add, p[h])
            acc_ref[h] = alpha[h] * acc_ref[h] + pv[h]
            m_ref[h] = m_new[h]

    def body(j, carry):
        block(j, False)
        return carry

    lax.fori_loop(0, qi, body, 0)
    block(qi, True)
    for h in range(MLA_HEADS):
        o_ref[:, h * MLA_V:(h + 1) * MLA_V] = acc_ref[h] / jnp.sum(l_ref[h], axis=-1, keepdims=True)


def _attention(q, k, v):
    B, S, _ = q.shape
    t = min(S, 512)
    H = MLA_HEADS
    return pl.pallas_call(
        functools.partial(_attn_kernel, t=t),
        out_shape=jax.ShapeDtypeStruct((B, S, MLA_DIM), F32),
        grid=(B, S // t),
        in_specs=[pl.BlockSpec((None, t, H * QK_W), lambda b, qi: (b, qi, 0)),
                  pl.BlockSpec((None, S, H * QK_W), lambda b, qi: (b, 0, 0)),
                  pl.BlockSpec((None, S, MLA_DIM), lambda b, qi: (b, 0, 0))],
        out_specs=pl.BlockSpec((None, t, MLA_DIM), lambda b, qi: (b, qi, 0)),
        scratch_shapes=[pltpu.VMEM((H, t, LANES), F32), pltpu.VMEM((H, t, LANES), F32),
                        pltpu.VMEM((H, t, MLA_V), F32)],
        compiler_params=_cparams("parallel", "arbitrary"),
        name="mla_attention",
    )(q, k, v)


def _merge_kernel(ya_ref, yb_ref, yc_ref, ga_ref, gb_ref, gc_ref, x_ref, g1_ref,
                  pa_ref, pb_ref, pc_ref, wo_ref, o_ref):
    m = (_sigmoid(ga_ref[...]) * _dot(ya_ref[...], pa_ref[...])
         + _sigmoid(gb_ref[...]) * _dot(yb_ref[...], pb_ref[...])
         + _sigmoid(gc_ref[...]) * _dot(yc_ref[...], pc_ref[...]))
    o_ref[...] = x_ref[...] + g1_ref[...] * _dot(m, wo_ref[...])


def _merge(ya, yb, yc, proj, x, g1, pa, pb, pc, wo):
    B, S, D = x.shape
    tm = min(S, 512)
    W = ya.shape[-1]
    row = lambda b, i: (b, i, 0)
    const = lambda b, i: (0, 0)
    gate = lambda n: pl.BlockSpec((None, tm, D), lambda b, i: (b, i, COL_GATES // D + n))
    return pl.pallas_call(
        _merge_kernel,
        out_shape=jax.ShapeDtypeStruct((B, S, D), F32),
        grid=(B, S // tm),
        in_specs=[pl.BlockSpec((None, tm, W), row), pl.BlockSpec((None, tm, W), row),
                  pl.BlockSpec((None, tm, W), row), gate(0), gate(1), gate(2),
                  pl.BlockSpec((None, tm, D), row),
                  pl.BlockSpec((None, 1, D), lambda b, i: (b, 0, 0)),
                  pl.BlockSpec((W, D), const), pl.BlockSpec((W, D), const), pl.BlockSpec((W, D), const),
                  pl.BlockSpec((D, D), const)],
        out_specs=pl.BlockSpec((None, tm, D), row),
        compiler_params=_cparams("parallel", "parallel"),
        name="branch_merge",
    )(ya, yb, yc, proj, proj, proj, x, g1, pa, pb, pc, wo)


def _ffn_kernel(x_ref, sc_ref, sh_ref, g2_ref, wg_ref, wu_ref, wd_ref, o_ref, h_ref, acc_ref):
    f = pl.program_id(2)

    @pl.when(f == 0)
    def _():
        h = _rms(x_ref[...]) * (1.0 + sc_ref[...]) + sh_ref[...]
        h_ref[...] = h.astype(BF16)
        acc_ref[...] = jnp.zeros_like(acc_ref)

    h = h_ref[...]
    a = _silu(jnp.dot(h, wg_ref[...], preferred_element_type=F32)) \
        * jnp.dot(h, wu_ref[...], preferred_element_type=F32)
    acc_ref[...] += jnp.dot(a.astype(BF16), wd_ref[...], preferred_element_type=F32)

    @pl.when(f == pl.num_programs(2) - 1)
    def _():
        o_ref[...] = x_ref[...] + g2_ref[...] * acc_ref[...]


def _dense_ffn(x, sc, sh, g2, wg, wu, wd):
    B, S, D = x.shape
    F = wg.shape[1]
    tm = min(S, 1024)
    tf = 256
    row = lambda b, i, f: (b, i, 0)
    vec = lambda b, i, f: (b, 0, 0)
    return pl.pallas_call(
        _ffn_kernel,
        out_shape=jax.ShapeDtypeStruct((B, S, D), F32),
        grid=(B, S // tm, F // tf),
        in_specs=[pl.BlockSpec((None, tm, D), row),
                  pl.BlockSpec((None, 1, D), vec), pl.BlockSpec((None, 1, D), vec),
                  pl.BlockSpec((None, 1, D), vec),
                  pl.BlockSpec((D, tf), lambda b, i, f: (0, f)),
                  pl.BlockSpec((D, tf), lambda b, i, f: (0, f)),
                  pl.BlockSpec((tf, D), lambda b, i, f: (f, 0))],
        out_specs=pl.BlockSpec((None, tm, D), row),
        scratch_shapes=[pltpu.VMEM((tm, D), BF16), pltpu.VMEM((tm, D), F32)],
        compiler_params=_cparams("parallel", "parallel", "arbitrary"),
        name="dense_ffn",
    )(x, sc, sh, g2, wg, wu, wd)


def _moe_pre_kernel(x_ref, sc_ref, sh_ref, r_ref, h_ref, top_ref):
    h = _rms(x_ref[...]) * (1.0 + sc_ref[...]) + sh_ref[...]
    h_ref[...] = h
    logits = _dot_hi(h, r_ref[...])
    lane = lax.broadcasted_iota(jnp.int32, logits.shape, 1)
    lane_f = lane.astype(F32)
    logits = jnp.where(lane < N_EXPERTS, logits, -jnp.inf)
    m1 = jnp.max(logits, axis=-1, keepdims=True)
    i1 = jnp.min(jnp.where(logits == m1, lane_f, float(LANES)), axis=-1, keepdims=True)
    rest = jnp.where(lane_f == i1, -jnp.inf, logits)
    m2 = jnp.max(rest, axis=-1, keepdims=True)
    i2 = jnp.min(jnp.where(rest == m2, lane_f, float(LANES)), axis=-1, keepdims=True)
    e2 = jnp.exp(m2 - m1)
    w1 = 1.0 / (1.0 + e2)
    w2 = e2 / (1.0 + e2)
    top_ref[...] = jnp.where(lane == 0, i1, jnp.where(lane == 1, i2,
                             jnp.where(lane == 2, w1, jnp.where(lane == 3, w2, 0.0))))


def _moe_pre(x, sc, sh, router_pad):
    B, S, D = x.shape
    tm = min(S, 512)
    row = lambda b, i: (b, i, 0)
    vec = lambda b, i: (b, 0, 0)
    return pl.pallas_call(
        _moe_pre_kernel,
        out_shape=(jax.ShapeDtypeStruct((B, S, D), F32), jax.ShapeDtypeStruct((B, S, LANES), F32)),
        grid=(B, S // tm),
        in_specs=[pl.BlockSpec((None, tm, D), row), pl.BlockSpec((None, 1, D), vec),
                  pl.BlockSpec((None, 1, D), vec), pl.BlockSpec((D, LANES), lambda b, i: (0, 0))],
        out_specs=(pl.BlockSpec((None, tm, D), row), pl.BlockSpec((None, tm, LANES), row)),
        compiler_params=_cparams("parallel", "parallel"),
        name="moe_route",
    )(x, sc, sh, router_pad)


def _moe_gather_kernel(cur_ref, nxt_ref, h_ref, o_ref, buf_ref, sem):
    i, n = pl.program_id(0), pl.num_programs(0)

    def issue(idx_ref, slot):
        def body(r, carry):
            pltpu.make_async_copy(h_ref.at[pl.ds(idx_ref[0, 0, r], 1)], buf_ref.at[slot, pl.ds(r, 1)],
                                  sem.at[slot]).start()
            return carry
        lax.fori_loop(0, MOE_BLOCK, body, 0, unroll=8)

    @pl.when(i == 0)
    def _():
        issue(cur_ref, 0)

    @pl.when(i + 1 < n)
    def _():
        issue(nxt_ref, (i + 1) % 2)

    slot = i % 2
    pltpu.make_async_copy(h_ref.at[pl.ds(0, MOE_BLOCK)], buf_ref.at[slot], sem.at[slot]).wait()
    o_ref[...] = buf_ref[slot].astype(BF16)


def _moe_gather(h2d, slot_tok):
    T, D = h2d.shape
    n_blocks = slot_tok.shape[0] // MOE_BLOCK
    idx = slot_tok.reshape(n_blocks, 1, MOE_BLOCK)
    return pl.pallas_call(
        _moe_gather_kernel,
        out_shape=jax.ShapeDtypeStruct((n_blocks * MOE_BLOCK, D), BF16),
        grid=(n_blocks,),
        in_specs=[pl.BlockSpec((1, 1, MOE_BLOCK), lambda i: (i, 0, 0), memory_space=pltpu.SMEM),
                  pl.BlockSpec((1, 1, MOE_BLOCK), lambda i: (jnp.minimum(i + 1, n_blocks - 1), 0, 0),
                               memory_space=pltpu.SMEM),
                  pl.BlockSpec(memory_space=pl.ANY)],
        out_specs=pl.BlockSpec((MOE_BLOCK, D), lambda i: (i, 0)),
        scratch_shapes=[pltpu.VMEM((2, MOE_BLOCK, D), F32), pltpu.SemaphoreType.DMA((2,))],
        compiler_params=_cparams("arbitrary"),
        name="moe_gather",
    )(idx, idx, h2d)


def _moe_up_kernel(be_ref, x_ref, wg_ref, wu_ref, o_ref, wg_bf, wu_bf):
    b = pl.program_id(1)
    prev = be_ref[jnp.maximum(b - 1, 0)]

    @pl.when((b == 0) | (be_ref[b] != prev))
    def _():
        wg_bf[...] = wg_ref[...].astype(BF16)
        wu_bf[...] = wu_ref[...].astype(BF16)

    x = x_ref[...]
    a = _silu(jnp.dot(x, wg_bf[...], preferred_element_type=F32)) \
        * jnp.dot(x, wu_bf[...], preferred_element_type=F32)
    o_ref[...] = a.astype(BF16)


def _moe_up(x_slots, block_e, w_gate, w_up, layer):
    slots, D = x_slots.shape
    F = w_gate.shape[-1]
    tf = 896
    n_blocks = slots // MOE_BLOCK
    wspec = pl.BlockSpec((None, None, D, tf), lambda f, b, be: (layer, be[b], 0, f))
    return pl.pallas_call(
        _moe_up_kernel,
        out_shape=jax.ShapeDtypeStruct((slots, F), BF16),
        grid_spec=pltpu.PrefetchScalarGridSpec(
            num_scalar_prefetch=1,
            grid=(F // tf, n_blocks),
            in_specs=[pl.BlockSpec((MOE_BLOCK, D), lambda f, b, be: (b, 0)), wspec, wspec],
            out_specs=pl.BlockSpec((MOE_BLOCK, tf), lambda f, b, be: (b, f)),
            scratch_shapes=[pltpu.VMEM((D, tf), BF16), pltpu.VMEM((D, tf), BF16)]),
        compiler_params=_cparams("arbitrary", "arbitrary"),
        name="moe_gate_up",
    )(block_e, x_slots, w_gate, w_up)


def _moe_down_kernel(be_ref, a_ref, wd_ref, o_ref, wd_bf):
    b = pl.program_id(0)
    prev = be_ref[jnp.maximum(b - 1, 0)]

    @pl.when((b == 0) | (be_ref[b] != prev))
    def _():
        wd_bf[...] = wd_ref[...].astype(BF16)

    o_ref[...] = jnp.dot(a_ref[...], wd_bf[...], preferred_element_type=F32)


def _moe_down(a_slots, block_e, w_down, layer):
    slots, F = a_slots.shape
    D = w_down.shape[-1]
    n_blocks = slots // MOE_BLOCK
    return pl.pallas_call(
        _moe_down_kernel,
        out_shape=jax.ShapeDtypeStruct((slots, D), F32),
        grid_spec=pltpu.PrefetchScalarGridSpec(
            num_scalar_prefetch=1,
            grid=(n_blocks,),
            in_specs=[pl.BlockSpec((MOE_BLOCK, F), lambda b, be: (b, 0)),
                      pl.BlockSpec((None, None, F, D), lambda b, be: (layer, be[b], 0, 0))],
            out_specs=pl.BlockSpec((MOE_BLOCK, D), lambda b, be: (b, 0)),
            scratch_shapes=[pltpu.VMEM((F, D), BF16)]),
        compiler_params=_cparams("arbitrary"),
        name="moe_down",
    )(block_e, a_slots, w_down)


def _moe_combine_kernel(cur_ref, nxt_ref, x_ref, g2_ref, top_ref, y_ref, o_ref, buf_ref, sem, *, tm):
    i = pl.program_id(0) * pl.num_programs(1) + pl.program_id(1)
    n = pl.num_programs(0) * pl.num_programs(1)

    def issue(idx_ref, slot):
        def body(r, carry):
            for k in range(TOP_K):
                pltpu.make_async_copy(y_ref.at[pl.ds(idx_ref[0, 0, TOP_K * r + k], 1)],
                                      buf_ref.at[slot, k, pl.ds(r, 1)], sem.at[slot]).start()
            return carry
        lax.fori_loop(0, tm, body, 0, unroll=4)

    @pl.when(i == 0)
    def _():
        issue(cur_ref, 0)

    @pl.when(i + 1 < n)
    def _():
        issue(nxt_ref, (i + 1) % 2)

    slot = i % 2
    for k in range(TOP_K):
        pltpu.make_async_copy(y_ref.at[pl.ds(0, tm)], buf_ref.at[slot, k], sem.at[slot]).wait()
    top = top_ref[...]
    f = top[:, 2:3] * buf_ref[slot, 0] + top[:, 3:4] * buf_ref[slot, 1]
    o_ref[...] = x_ref[...] + g2_ref[...] * f


def _moe_combine(x, g2, top, dest, y_slots):
    B, S, D = x.shape
    tm = min(S, 256)
    nb = S // tm
    row = lambda b, i: (b, i, 0)
    idx = dest.reshape(B * nb, 1, TOP_K * tm)
    return pl.pallas_call(
        functools.partial(_moe_combine_kernel, tm=tm),
        out_shape=jax.ShapeDtypeStruct((B, S, D), F32),
        grid=(B, nb),
        in_specs=[pl.BlockSpec((1, 1, TOP_K * tm), lambda b, i: (b * nb + i, 0, 0), memory_space=pltpu.SMEM),
                  pl.BlockSpec((1, 1, TOP_K * tm), lambda b, i: (jnp.minimum(b * nb + i + 1, B * nb - 1), 0, 0),
                               memory_space=pltpu.SMEM),
                  pl.BlockSpec((None, tm, D), row),
                  pl.BlockSpec((None, 1, D), lambda b, i: (b, 0, 0)),
                  pl.BlockSpec((None, tm, LANES), row),
                  pl.BlockSpec(memory_space=pl.ANY)],
        out_specs=pl.BlockSpec((None, tm, D), row),
        scratch_shapes=[pltpu.VMEM((2, TOP_K, tm, D), F32), pltpu.SemaphoreType.DMA((2,))],
        compiler_params=_cparams("arbitrary", "arbitrary"),
        name="moe_combine",
    )(idx, idx, x, g2, top, y_slots)


def _moe_ffn(x, sc, sh, g2, router, w_gate, w_up, w_down, layer):
    B, S, D = x.shape
    T = B * S
    router_pad = jnp.zeros((D, LANES), F32).at[:, :N_EXPERTS].set(router)
    h, top = _moe_pre(x, sc, sh, router_pad)
    top2d = top.reshape(T, LANES)
    flat_e = top2d[:, :TOP_K].astype(jnp.int32).reshape(-1)
    n_assign = T * TOP_K
    n_blocks = (n_assign + N_EXPERTS * (MOE_BLOCK - 1)) // MOE_BLOCK
    slots = n_blocks * MOE_BLOCK
    onehot = (flat_e[:, None] == jnp.arange(N_EXPERTS, dtype=jnp.int32)[None, :]).astype(jnp.int32)
    csum = jnp.cumsum(onehot, axis=0)
    rank = jnp.sum(csum * onehot, axis=1) - 1
    counts = csum[-1]
    padded = (counts + MOE_BLOCK - 1) // MOE_BLOCK * MOE_BLOCK
    pad_end = jnp.cumsum(padded)
    start_padded = pad_end - padded
    dest = (start_padded[flat_e] + rank).astype(jnp.int32)
    flat_tok = jnp.repeat(jnp.arange(T, dtype=jnp.int32), TOP_K)
    slot_tok = jnp.zeros((slots,), jnp.int32).at[dest].set(flat_tok)
    block_e = jnp.minimum(jnp.searchsorted(pad_end, jnp.arange(n_blocks) * MOE_BLOCK, side='right'),
                          N_EXPERTS - 1).astype(jnp.int32)
    x_slots = _moe_gather(h.reshape(T, D), slot_tok)
    a_slots = _moe_up(x_slots, block_e, w_gate, w_up, layer)
    y_slots = _moe_down(a_slots, block_e, w_down, layer)
    return _moe_combine(x, g2, top, dest, y_slots)


def _final_norm_kernel(x_ref, w_ref, o_ref):
    o_ref[...] = _rms(x_ref[...]) * w_ref[...]


def _final_norm(x2d, w):
    T, D = x2d.shape
    tm = min(T, 1024)
    return pl.pallas_call(
        _final_norm_kernel,
        out_shape=jax.ShapeDtypeStruct((T, D), F32),
        grid=(T // tm,),
        in_specs=[pl.BlockSpec((tm, D), lambda i: (i, 0)), pl.BlockSpec((1, D), lambda i: (0, 0))],
        out_specs=pl.BlockSpec((tm, D), lambda i: (i, 0)),
        compiler_params=_cparams("parallel"),
        name="final_norm",
    )(x2d, w.reshape(1, D))


def _arrange_w_in(w):
    D = w.shape[0]
    o = np.cumsum([0, 3 * GDN_DIM, GDN_DIM, GDN_HEADS, GDN_HEADS, SSM_INNER, SSM_CONV_DIM, SSM_HEADS,
                   MLA_Q_LORA, MLA_KV_LORA, MLA_ROPE, 3 * w.shape[0]])
    seg = lambda i: w[:, int(o[i]):int(o[i + 1])]
    qkv, gz, ga, gb, sz, xbc, dt, cq, ckv, kr, gates = [seg(i) for i in range(11)]
    half = MLA_ROPE // 2
    small = jnp.concatenate([ga, gb, dt, jnp.zeros((D, LANES - 2 * GDN_HEADS - SSM_HEADS), w.dtype)], axis=1)
    kr_full = jnp.concatenate([kr, -kr[:, half:], kr[:, :half]], axis=1)
    out = jnp.concatenate([qkv, gz, sz, cq, xbc, gates, ckv, small, kr_full], axis=1)
    assert out.shape[1] == IN_COLS
    return out.astype(BF16)


def _arrange_w_uq(w):
    half = MLA_ROPE // 2
    parts = []
    for h in range(MLA_HEADS):
        c0 = h * (MLA_NOPE + MLA_ROPE)
        rope = w[:, c0 + MLA_NOPE:c0 + MLA_NOPE + MLA_ROPE]
        parts += [w[:, c0:c0 + MLA_NOPE], rope, -rope[:, half:], rope[:, :half]]
    return jnp.concatenate(parts, axis=1).astype(BF16)


def kernel(x, c, positions, w_ada, b_ada, w_in, gdn_conv_w, gdn_a_log, gdn_dt_bias, gdn_norm_w, ssm_conv_w, ssm_conv_b, ssm_a_log, ssm_dt_bias, ssm_d, ssm_norm_w, mla_q_norm_w, mla_w_uq, mla_kv_norm_w, mla_w_uk, mla_w_uv, w_branch_a, w_branch_b, w_branch_c, w_out, ffn_w_gate, ffn_w_up, ffn_w_down, moe_router, moe_w_gate, moe_w_up, moe_w_down, final_norm_w):
    B, S, D = x.shape
    T = B * S
    depth = w_ada.shape[0]
    c_pad = jnp.zeros((SUBLANES, D), F32).at[:B].set(c)
    mod = _modulation(c_pad, w_ada.astype(BF16), b_ada)[:, :B].reshape(depth, B, 6, 1, D)
    cos, sin = _rope_tables(positions)
    for l in range(depth):
        sh1, sc1, g1, sh2, sc2, g2 = [mod[l, :, n] for n in range(6)]
        proj = _in_proj(x, sc1, sh1, _arrange_w_in(w_in[l]))
        y_a = _gdn(proj, gdn_conv_w[l], gdn_a_log[l], gdn_dt_bias[l], gdn_norm_w[l])
        y_b = _ssd(proj, ssm_conv_w[l], ssm_conv_b[l], ssm_a_log[l], ssm_dt_bias[l], ssm_d[l], ssm_norm_w[l])
        q, k, v = _mla_proj(proj.reshape(T, IN_COLS), cos, sin, mla_q_norm_w[l], _arrange_w_uq(mla_w_uq[l]),
                            mla_kv_norm_w[l], mla_w_uk[l].astype(BF16), mla_w_uv[l].astype(BF16))
        y_c = _attention(q.reshape(B, S, -1), k.reshape(B, S, -1), v.reshape(B, S, -1))
        x = _merge(y_a, y_b, y_c, proj, x, g1, w_branch_a[l].astype(BF16), w_branch_b[l].astype(BF16),
                   w_branch_c[l].astype(BF16), w_out[l].astype(BF16))
        i = l // 2
        if l % 2 == 0:
            x = _dense_ffn(x, sc2, sh2, g2, ffn_w_gate[i].astype(BF16), ffn_w_up[i].astype(BF16),
                           ffn_w_down[i].astype(BF16))
        else:
            x = _moe_ffn(x, sc2, sh2, g2, moe_router[i], moe_w_gate, moe_w_up, moe_w_down, i)
    return _final_norm(x.reshape(T, D), final_norm_w).reshape(B, S, D)
```

```python
import functools
import math

import numpy as np
import jax
import jax.numpy as jnp
from jax import lax
from jax.experimental import pallas as pl
from jax.experimental.pallas import tpu as pltpu

F32 = jnp.float32
BF16 = jnp.bfloat16
HI = lax.Precision.HIGHEST

EPS = 1e-6
CHUNK = 64
CONV_K = 4
GDN_HEADS = 4
GDN_HEAD_DIM = 128
GDN_DIM = GDN_HEADS * GDN_HEAD_DIM
SSM_HEADS = 8
SSM_HEAD_DIM = 64
SSM_INNER = SSM_HEADS * SSM_HEAD_DIM
SSM_GROUPS = 2
SSM_STATE = 128
SSM_CONV_DIM = SSM_INNER + 2 * SSM_GROUPS * SSM_STATE
MLA_HEADS = 4
MLA_Q_LORA = 512
MLA_KV_LORA = 256
MLA_NOPE = 128
MLA_ROPE = 64
MLA_V = 128
MLA_DIM = MLA_HEADS * MLA_V
ROPE_THETA = 10000.0
N_EXPERTS = 8
TOP_K = 2
MOE_BLOCK = 1024

LANES = 128
SUBLANES = 8
VMEM_LIMIT = 56 * 1024 * 1024

COL_QKV = 0
COL_GZ = 1536
COL_SZ = 2048
COL_CQ = 2560
COL_XBC = 3072
COL_GATES = 4096
COL_CKV = 7168
COL_SMALL = 7424
COL_KR = 7552
IN_COLS = 7680


def _cparams(*sem):
    return pltpu.CompilerParams(dimension_semantics=tuple(sem), vmem_limit_bytes=VMEM_LIMIT)


def _silu(x):
    return x * (1.0 / (1.0 + jnp.exp(-x)))


def _sigmoid(x):
    return 1.0 / (1.0 + jnp.exp(-x))


def _softplus(x):
    return jnp.maximum(x, 0.0) + jnp.log(1.0 + jnp.exp(-jnp.abs(x)))


def _rms(x):
    return x * lax.rsqrt(jnp.mean(x * x, axis=-1, keepdims=True) + EPS)


def _dot(a, b):
    return jnp.dot(a.astype(BF16), b.astype(BF16), preferred_element_type=F32)


def _dot_nt(a, b):
    return lax.dot_general(a.astype(BF16), b.astype(BF16), (((1,), (1,)), ((), ())),
                           preferred_element_type=F32)


def _dot_tn(a, b):
    return lax.dot_general(a.astype(BF16), b.astype(BF16), (((0,), (0,)), ((), ())),
                           preferred_element_type=F32)


def _dot_hi(a, b):
    return jnp.dot(a, b, precision=HI, preferred_element_type=F32)


def _mod_kernel(c_ref, w_ref, b_ref, o_ref):
    cond = _silu(c_ref[...])
    o_ref[...] = _dot(cond, w_ref[...]) + b_ref[...]


def _modulation(c_pad, w_ada, b_ada):
    L, D, N = w_ada.shape
    tn = 1536
    return pl.pallas_call(
        _mod_kernel,
        out_shape=jax.ShapeDtypeStruct((L, SUBLANES, N), F32),
        grid=(L, N // tn),
        in_specs=[pl.BlockSpec((SUBLANES, D), lambda l, j: (0, 0)),
                  pl.BlockSpec((None, D, tn), lambda l, j: (l, 0, j)),
                  pl.BlockSpec((None, 1, tn), lambda l, j: (l, 0, j))],
        out_specs=pl.BlockSpec((None, SUBLANES, tn), lambda l, j: (l, 0, j)),
        compiler_params=_cparams("parallel", "parallel"),
        name="adaln_mod",
    )(c_pad, w_ada, b_ada.reshape(L, 1, N))


def _rope_table_kernel(pos_ref, freq_ref, cos_ref, sin_ref):
    ang = pos_ref[...].astype(F32) * freq_ref[...]
    keep = lax.broadcasted_iota(jnp.int32, ang.shape, 1) < MLA_ROPE
    cos_ref[...] = jnp.where(keep, jnp.cos(ang), 0.0)
    sin_ref[...] = jnp.where(keep, jnp.sin(ang), 0.0)


def _rope_tables(positions):
    T = positions.size
    tm = min(T, 1024)
    inv_freq = ROPE_THETA ** (-jnp.arange(0, MLA_ROPE, 2, dtype=F32) / MLA_ROPE)
    freq = jnp.concatenate([inv_freq, inv_freq, jnp.zeros((LANES - MLA_ROPE,), F32)]).reshape(1, LANES)
    return pl.pallas_call(
        _rope_table_kernel,
        out_shape=(jax.ShapeDtypeStruct((T, LANES), F32), jax.ShapeDtypeStruct((T, LANES), F32)),
        grid=(T // tm,),
        in_specs=[pl.BlockSpec((tm, 1), lambda i: (i, 0)),
                  pl.BlockSpec((1, LANES), lambda i: (0, 0))],
        out_specs=(pl.BlockSpec((tm, LANES), lambda i: (i, 0)),
                   pl.BlockSpec((tm, LANES), lambda i: (i, 0))),
        compiler_params=_cparams("parallel"),
        name="rope_tables",
    )(positions.reshape(T, 1), freq)


def _in_proj_kernel(x_ref, sc_ref, sh_ref, w_ref, o_ref, h_ref):
    @pl.when(pl.program_id(2) == 0)
    def _():
        h = _rms(x_ref[...]) * (1.0 + sc_ref[...]) + sh_ref[...]
        h_ref[...] = h.astype(BF16)

    o_ref[...] = jnp.dot(h_ref[...], w_ref[...], preferred_element_type=F32)


def _in_proj(x, sc, sh, w):
    B, S, D = x.shape
    N = w.shape[1]
    tm = min(S, 1024)
    tn = 1536
    return pl.pallas_call(
        _in_proj_kernel,
        out_shape=jax.ShapeDtypeStruct((B, S, N), F32),
        grid=(B, S // tm, N // tn),
        in_specs=[pl.BlockSpec((None, tm, D), lambda b, i, j: (b, i, 0)),
                  pl.BlockSpec((None, 1, D), lambda b, i, j: (b, 0, 0)),
                  pl.BlockSpec((None, 1, D), lambda b, i, j: (b, 0, 0)),
                  pl.BlockSpec((D, tn), lambda b, i, j: (0, j))],
        out_specs=pl.BlockSpec((None, tm, tn), lambda b, i, j: (b, i, j)),
        scratch_shapes=[pltpu.VMEM((tm, D), BF16)],
        compiler_params=_cparams("parallel", "parallel", "arbitrary"),
        name="in_proj",
    )(x, sc, sh, w)


def _causal_conv(x, tail_ref, w):
    rows = x.shape[0]
    xe = jnp.concatenate([tail_ref[...], x], axis=0)
    tail_ref[...] = x[rows - SUBLANES:, :]
    y = xe[SUBLANES:, :] * w[CONV_K - 1:CONV_K, :]
    for j in range(CONV_K - 1):
        shift = CONV_K - 1 - j
        y = y + pltpu.roll(xe, shift, axis=0)[SUBLANES:, :] * w[j:j + 1, :]
    return y


def _chunk_cumsum(v, chunk=CHUNK):
    pos = lax.broadcasted_iota(jnp.int32, v.shape, 0) % chunk
    step = 1
    while step < chunk:
        v = v + jnp.where(pos >= step, pltpu.roll(v, step, axis=0), 0.0)
        step *= 2
    return v


def _col_to_row(col_b, eye):
    return jnp.sum(jnp.where(eye, col_b, 0.0), axis=0, keepdims=True)


GDN_CHUNK = 128
GDN_PASSES = 3


def _split(a):
    hi = a.astype(BF16)
    return hi, (a - hi.astype(F32)).astype(BF16)


def _mm_split(a, b):
    d = lambda x, y: jnp.dot(x, y, preferred_element_type=F32)
    if GDN_PASSES == 1:
        return d(a[0], b[0])
    return d(a[0], b[0]) + (d(a[0], b[1]) + d(a[1], b[0]))


def _gdn_kernel(qkv_ref, z_ref, sm_ref, cw_ref, par_ref, nw_ref, o_ref,
                tail_ref, state_ref, *, rows):
    C, H, Dh = GDN_CHUNK, GDN_HEADS, GDN_HEAD_DIM
    n_ck = rows // C

    @pl.when(pl.program_id(1) == 0)
    def _():
        tail_ref[...] = jnp.zeros_like(tail_ref)
        state_ref[...] = jnp.zeros_like(state_ref)

    y = _silu(_causal_conv(qkv_ref[...], tail_ref, cw_ref[...]))
    sm = sm_ref[...]
    neg_a = par_ref[0:1, :]
    dt_b = par_ref[1:2, :]
    g_all = _chunk_cumsum(neg_a * _softplus(sm + dt_b), C)
    beta_all = _sigmoid(sm)
    z = z_ref[...]
    nw = nw_ref[...]

    ri = lax.broadcasted_iota(jnp.int32, (C, C), 0)
    ci = lax.broadcasted_iota(jnp.int32, (C, C), 1)
    eye = ri == ci
    incl = ri >= ci
    strict = ri > ci
    eye_f = jnp.where(eye, 1.0, 0.0)

    qn, kn, vn = [], [], []
    for h in range(H):
        q_h = y[:, h * Dh:(h + 1) * Dh]
        k_h = y[:, GDN_DIM + h * Dh:GDN_DIM + (h + 1) * Dh]
        qn.append(q_h * lax.rsqrt(jnp.sum(q_h * q_h, axis=-1, keepdims=True) + EPS) * (Dh ** -0.5))
        kn.append(k_h * lax.rsqrt(jnp.sum(k_h * k_h, axis=-1, keepdims=True) + EPS))
        vn.append(y[:, 2 * GDN_DIM + h * Dh:2 * GDN_DIM + (h + 1) * Dh])

    items = [(c, h) for c in range(n_ck) for h in range(H)]
    p, rhs, qk, q_dec, k_dec, g_end = {}, {}, {}, {}, {}, {}
    for c, h in items:
        r0 = c * C
        q_c, k_c, v_c = qn[h][r0:r0 + C], kn[h][r0:r0 + C], vn[h][r0:r0 + C]
        g_col = g_all[r0:r0 + C, h:h + 1]
        beta = beta_all[r0:r0 + C, H + h:H + h + 1]
        g_b = jnp.broadcast_to(g_col, (C, C))
        g_last = g_col[C - 1:C, :]
        decay = jnp.exp(jnp.where(incl, g_b - _col_to_row(g_b, eye), -1e30))
        eg = jnp.exp(g_col)
        p[c, h] = jnp.where(strict, -(beta * _dot_nt(k_c, k_c) * decay), 0.0)
        qk[c, h] = (_dot_nt(q_c, k_c) * decay).astype(BF16)
        rhs[c, h] = _split(jnp.concatenate([v_c * beta, k_c * (beta * eg)], axis=-1))
        q_dec[c, h] = (q_c * eg).astype(BF16)
        k_dec[c, h] = (k_c * jnp.exp(g_last - g_col)).astype(BF16)
        g_end[c, h] = jnp.exp(g_last)

    inv = {it: eye_f + jnp.where((ri // 2 == ci // 2) & strict, p[it], 0.0) for it in items}
    s = 2
    while s < C:
        level = ((ri // s) % 2 == 1) & ((ci // s) % 2 == 0) & (ri // (2 * s) == ci // (2 * s))
        t_split = {it: _split(inv[it]) for it in items}
        x = {it: _mm_split(_split(jnp.where(level, p[it], 0.0)), t_split[it]) for it in items}
        inv = {it: inv[it] + _mm_split(t_split[it], _split(x[it])) for it in items}
        s *= 2
    sol = {it: _mm_split(_split(inv[it]), rhs[it]) for it in items}

    states = [state_ref[h] for h in range(H)]
    for c in range(n_ck):
        r0 = c * C
        s_bf = [s.astype(BF16) for s in states]
        v_new = [sol[c, h][:, :Dh] - _dot(sol[c, h][:, Dh:], s_bf[h]) for h in range(H)]
        v_bf = [v.astype(BF16) for v in v_new]
        outs = [jnp.dot(q_dec[c, h], s_bf[h], preferred_element_type=F32)
                + jnp.dot(qk[c, h], v_bf[h], preferred_element_type=F32) for h in range(H)]
        states = [states[h] * g_end[c, h]
                  + lax.dot_general(k_dec[c, h], v_bf[h], (((0,), (0,)), ((), ())), preferred_element_type=F32)
                  for h in range(H)]
        for h in range(H):
            z_c = z[r0:r0 + C, h * Dh:(h + 1) * Dh]
            o_ref[r0:r0 + C, h * Dh:(h + 1) * Dh] = _rms(outs[h]) * nw * _silu(z_c)
    for h in range(H):
        state_ref[h] = states[h]


def _gdn(proj, conv_w, a_log, dt_bias, norm_w):
    B, S, _ = proj.shape
    rows = min(S, 512)
    par = jnp.zeros((SUBLANES, LANES), F32)
    par = par.at[0, :GDN_HEADS].set(-jnp.exp(a_log)).at[1, :GDN_HEADS].set(dt_bias)
    w3 = 3 * GDN_DIM
    return pl.pallas_call(
        functools.partial(_gdn_kernel, rows=rows),
        out_shape=jax.ShapeDtypeStruct((B, S, GDN_DIM), F32),
        grid=(B, S // rows),
        in_specs=[pl.BlockSpec((None, rows, w3), lambda b, i: (b, i, COL_QKV // w3)),
                  pl.BlockSpec((None, rows, GDN_DIM), lambda b, i: (b, i, COL_GZ // GDN_DIM)),
                  pl.BlockSpec((None, rows, LANES), lambda b, i: (b, i, COL_SMALL // LANES)),
                  pl.BlockSpec((CONV_K, w3), lambda b, i: (0, 0)),
                  pl.BlockSpec((SUBLANES, LANES), lambda b, i: (0, 0)),
                  pl.BlockSpec((1, GDN_HEAD_DIM), lambda b, i: (0, 0))],
        out_specs=pl.BlockSpec((None, rows, GDN_DIM), lambda b, i: (b, i, 0)),
        scratch_shapes=[pltpu.VMEM((SUBLANES, w3), F32),
                        pltpu.VMEM((GDN_HEADS, GDN_HEAD_DIM, GDN_HEAD_DIM), F32)],
        compiler_params=_cparams("parallel", "arbitrary"),
        name="gated_delta_net",
    )(proj, proj, proj, conv_w, par, norm_w.reshape(1, GDN_HEAD_DIM))


SSD_CHUNK = 128


def _ssd_kernel(xbc_ref, z_ref, sm_ref, cw_ref, cb_ref, par_ref, nw_ref, o_ref,
                tail_ref, state_ref, *, rows):
    C, H, P, G, N = SSD_CHUNK, SSM_HEADS, SSM_HEAD_DIM, SSM_GROUPS, SSM_STATE
    E = H // G
    GW = E * P

    @pl.when(pl.program_id(1) == 0)
    def _():
        tail_ref[...] = jnp.zeros_like(tail_ref)
        state_ref[...] = jnp.zeros_like(state_ref)

    y = _silu(_causal_conv(xbc_ref[...], tail_ref, cw_ref[...]) + cb_ref[...])
    sm = sm_ref[...]
    neg_a = par_ref[0:1, :]
    dt_b = par_ref[1:2, :]
    d_skip = par_ref[2:3, :]
    dt_all = _softplus(sm + dt_b)
    acs_all = _chunk_cumsum(dt_all * neg_a, C)
    z = z_ref[...]
    nw = nw_ref[...]

    ri = lax.broadcasted_iota(jnp.int32, (C, C), 0)
    ci = lax.broadcasted_iota(jnp.int32, (C, C), 1)
    eye = ri == ci
    incl = ri >= ci

    items = [(c, g) for c in range(rows // C) for g in range(G)]
    cm, y_local, st_add, e_in, e_out = {}, {}, {}, {}, {}
    for c, g in items:
        r0 = c * C
        bm_c = y[r0:r0 + C, SSM_INNER + g * N:SSM_INNER + (g + 1) * N]
        cm_c = y[r0:r0 + C, SSM_INNER + G * N + g * N:SSM_INNER + G * N + (g + 1) * N]
        x_c = y[r0:r0 + C, g * GW:(g + 1) * GW]
        cb = _dot_nt(cm_c, bm_c)
        xs_parts, ea_parts, ydiag_parts, dx_parts, last_parts = [], [], [], [], []
        for e in range(E):
            lane = SUBLANES + g * E + e
            a_col = acs_all[r0:r0 + C, lane:lane + 1]
            dt_col = dt_all[r0:r0 + C, lane:lane + 1]
            a_b = jnp.broadcast_to(a_col, (C, C))
            a_last = a_col[C - 1:C, :]
            L = jnp.exp(jnp.where(incl, a_b - _col_to_row(a_b, eye), -1e30))
            x_e = x_c[:, e * P:(e + 1) * P]
            xd = x_e * dt_col
            ydiag_parts.append(_dot(cb * L, xd))
            xs_parts.append(xd * jnp.exp(a_last - a_col))
            ea_parts.append(jnp.broadcast_to(jnp.exp(a_col), (C, P)))
            last_parts.append(jnp.broadcast_to(jnp.exp(a_last), (1, P)))
            dx_parts.append(x_e * d_skip[:, lane:lane + 1])
        cm[c, g] = cm_c.astype(BF16)
        y_local[c, g] = jnp.concatenate(ydiag_parts, axis=-1) + jnp.concatenate(dx_parts, axis=-1)
        st_add[c, g] = _dot_tn(bm_c, jnp.concatenate(xs_parts, axis=-1))
        e_in[c, g] = jnp.concatenate(ea_parts, axis=-1)
        e_out[c, g] = jnp.concatenate(last_parts, axis=-1)

    states = [state_ref[g] for g in range(G)]
    for c in range(rows // C):
        r0 = c * C
        for g in range(G):
            y_off = jnp.dot(cm[c, g], states[g].astype(BF16), preferred_element_type=F32)
            y_c = (y_local[c, g] + y_off * e_in[c, g]) * _silu(z[r0:r0 + C, g * GW:(g + 1) * GW])
            o_ref[r0:r0 + C, g * GW:(g + 1) * GW] = _rms(y_c) * nw[:, g * GW:(g + 1) * GW]
            states[g] = states[g] * e_out[c, g] + st_add[c, g]
    for g in range(G):
        state_ref[g] = states[g]


def _ssd(proj, conv_w, conv_b, a_log, dt_bias, d_skip, norm_w):
    B, S, _ = proj.shape
    rows = min(S, 512)
    par = jnp.zeros((SUBLANES, LANES), F32)
    sl = slice(SUBLANES, SUBLANES + SSM_HEADS)
    par = par.at[0, sl].set(-jnp.exp(a_log)).at[1, sl].set(dt_bias).at[2, sl].set(d_skip)
    return pl.pallas_call(
        functools.partial(_ssd_kernel, rows=rows),
        out_shape=jax.ShapeDtypeStruct((B, S, SSM_INNER), F32),
        grid=(B, S // rows),
        in_specs=[pl.BlockSpec((None, rows, SSM_CONV_DIM), lambda b, i: (b, i, COL_XBC // SSM_CONV_DIM)),
                  pl.BlockSpec((None, rows, SSM_INNER), lambda b, i: (b, i, COL_SZ // SSM_INNER)),
                  pl.BlockSpec((None, rows, LANES), lambda b, i: (b, i, COL_SMALL // LANES)),
                  pl.BlockSpec((CONV_K, SSM_CONV_DIM), lambda b, i: (0, 0)),
                  pl.BlockSpec((1, SSM_CONV_DIM), lambda b, i: (0, 0)),
                  pl.BlockSpec((SUBLANES, LANES), lambda b, i: (0, 0)),
                  pl.BlockSpec((1, SSM_INNER), lambda b, i: (0, 0))],
        out_specs=pl.BlockSpec((None, rows, SSM_INNER), lambda b, i: (b, i, 0)),
        scratch_shapes=[pltpu.VMEM((SUBLANES, SSM_CONV_DIM), F32),
                        pltpu.VMEM((SSM_GROUPS, SSM_STATE, SSM_INNER // SSM_GROUPS), F32)],
        compiler_params=_cparams("parallel", "arbitrary"),
        name="mamba2_ssd",
    )(proj, proj, proj, conv_w, conv_b.reshape(1, -1), par, norm_w.reshape(1, -1))


QK_W = 2 * LANES


def _rope(x, cos, sin):
    return x * cos + pltpu.roll(x, MLA_ROPE, axis=1) * sin


def _mla_proj_kernel(cq_ref, ckv_ref, kr_ref, cos_ref, sin_ref, qw_ref, wq_ref, kvw_ref, wk_ref, wv_ref,
                     q_ref, k_ref, v_ref):
    scale = (MLA_NOPE + MLA_ROPE) ** -0.5
    cos, sin = cos_ref[...], sin_ref[...]
    qc = _rms(cq_ref[...]) * qw_ref[...]
    qf = _dot(qc, wq_ref[...]) * scale
    ckv = (_rms(ckv_ref[...]) * kvw_ref[...]).astype(BF16)
    kn = jnp.dot(ckv, wk_ref[...], preferred_element_type=F32)
    v_ref[...] = jnp.dot(ckv, wv_ref[...], preferred_element_type=F32).astype(BF16)
    kr = _rope(kr_ref[...], cos, sin).astype(BF16)
    for h in range(MLA_HEADS):
        c0 = h * QK_W
        q_ref[:, c0:c0 + LANES] = qf[:, c0:c0 + LANES].astype(BF16)
        q_ref[:, c0 + LANES:c0 + QK_W] = _rope(qf[:, c0 + LANES:c0 + QK_W], cos, sin).astype(BF16)
        k_ref[:, c0:c0 + LANES] = kn[:, h * MLA_NOPE:(h + 1) * MLA_NOPE].astype(BF16)
        k_ref[:, c0 + LANES:c0 + QK_W] = kr


def _mla_proj(proj2d, cos, sin, q_norm_w, wq, kv_norm_w, wk, wv):
    T = proj2d.shape[0]
    tm = min(T, 512)
    HQ = MLA_HEADS * QK_W
    row = lambda i: (i, 0)
    const = lambda i: (0, 0)
    return pl.pallas_call(
        _mla_proj_kernel,
        out_shape=(jax.ShapeDtypeStruct((T, HQ), BF16), jax.ShapeDtypeStruct((T, HQ), BF16),
                   jax.ShapeDtypeStruct((T, MLA_DIM), BF16)),
        grid=(T // tm,),
        in_specs=[pl.BlockSpec((tm, MLA_Q_LORA), lambda i: (i, COL_CQ // MLA_Q_LORA)),
                  pl.BlockSpec((tm, MLA_KV_LORA), lambda i: (i, COL_CKV // MLA_KV_LORA)),
                  pl.BlockSpec((tm, LANES), lambda i: (i, COL_KR // LANES)),
                  pl.BlockSpec((tm, LANES), row),
                  pl.BlockSpec((tm, LANES), row),
                  pl.BlockSpec((1, MLA_Q_LORA), const),
                  pl.BlockSpec((MLA_Q_LORA, HQ), const),
                  pl.BlockSpec((1, MLA_KV_LORA), const),
                  pl.BlockSpec((MLA_KV_LORA, MLA_HEADS * MLA_NOPE), const),
                  pl.BlockSpec((MLA_KV_LORA, MLA_DIM), const)],
        out_specs=(pl.BlockSpec((tm, HQ), row), pl.BlockSpec((tm, HQ), row),
                   pl.BlockSpec((tm, MLA_DIM), row)),
        compiler_params=_cparams("parallel"),
        name="mla_proj",
    )(proj2d, proj2d, proj2d, cos, sin, q_norm_w.reshape(1, -1), wq, kv_norm_w.reshape(1, -1), wk, wv)


def _attn_kernel(q_ref, k_ref, v_ref, o_ref, m_ref, l_ref, acc_ref, *, t):
    qi = pl.program_id(1)
    m_ref[...] = jnp.full_like(m_ref, -1e30)
    l_ref[...] = jnp.zeros_like(l_ref)
    acc_ref[...] = jnp.zeros_like(acc_ref)

    def block(j, diagonal):
        k0 = pl.multiple_of(j * t, t)
        hs = range(MLA_HEADS)
        s = [lax.dot_general(q_ref[:, h * QK_W:(h + 1) * QK_W], k_ref[pl.ds(k0, t), h * QK_W:(h + 1) * QK_W],
                             (((1,), (1,)), ((), ())), preferred_element_type=F32) for h in hs]
        if diagonal:
            row = lax.broadcasted_iota(jnp.int32, s[0].shape, 0)
            col = lax.broadcasted_iota(jnp.int32, s[0].shape, 1)
            s = [jnp.where(row >= col, s[h], -1e30) for h in hs]
        cols = range(t // LANES)
        sc = [[s[h][:, c * LANES:(c + 1) * LANES] for c in cols] for h in hs]
        m_prev = [m_ref[h] for h in hs]
        m_new = [jnp.maximum(m_prev[h], jnp.max(functools.reduce(jnp.maximum, sc[h]), axis=-1, keepdims=True))
                 for h in hs]
        p = [[jnp.exp(sc[h][c] - m_new[h]) for c in cols] for h in hs]
        alpha = [jnp.exp(m_prev[h] - m_new[h]) for h in hs]
        pv = [jnp.dot(jnp.concatenate(p[h], axis=-1).astype(BF16), v_ref[pl.ds(k0, t), h * MLA_V:(h + 1) * MLA_V],
                      preferred_element_type=F32) for h in hs]
        for h in hs:
            l_ref[h] = alpha[h] * l_ref[h] + functools.reduce(jnp.add, p[h])
            acc_ref[h] = alpha[h] * acc_ref[h] + pv[h]
            m_ref[h] = m_new[h]

    def body(j, carry):
        block(j, False)
        return carry

    lax.fori_loop(0, qi, body, 0)
    block(qi, True)
    for h in range(MLA_HEADS):
        o_ref[:, h * MLA_V:(h + 1) * MLA_V] = acc_ref[h] / jnp.sum(l_ref[h], axis=-1, keepdims=True)


def _attention(q, k, v):
    B, S, _ = q.shape
    t = min(S, 512)
    H = MLA_HEADS
    return pl.pallas_call(
        functools.partial(_attn_kernel, t=t),
        out_shape=jax.ShapeDtypeStruct((B, S, MLA_DIM), F32),
        grid=(B, S // t),
        in_specs=[pl.BlockSpec((None, t, H * QK_W), lambda b, qi: (b, qi, 0)),
                  pl.BlockSpec((None, S, H * QK_W), lambda b, qi: (b, 0, 0)),
                  pl.BlockSpec((None, S, MLA_DIM), lambda b, qi: (b, 0, 0))],
        out_specs=pl.BlockSpec((None, t, MLA_DIM), lambda b, qi: (b, qi, 0)),
        scratch_shapes=[pltpu.VMEM((H, t, LANES), F32), pltpu.VMEM((H, t, LANES), F32),
                        pltpu.VMEM((H, t, MLA_V), F32)],
        compiler_params=_cparams("parallel", "arbitrary"),
        name="mla_attention",
    )(q, k, v)


def _merge_kernel(ya_ref, yb_ref, yc_ref, ga_ref, gb_ref, gc_ref, x_ref, g1_ref,
                  pa_ref, pb_ref, pc_ref, wo_ref, o_ref):
    m = (_sigmoid(ga_ref[...]) * _dot(ya_ref[...], pa_ref[...])
         + _sigmoid(gb_ref[...]) * _dot(yb_ref[...], pb_ref[...])
         + _sigmoid(gc_ref[...]) * _dot(yc_ref[...], pc_ref[...]))
    o_ref[...] = x_ref[...] + g1_ref[...] * _dot(m, wo_ref[...])


def _merge(ya, yb, yc, proj, x, g1, pa, pb, pc, wo):
    B, S, D = x.shape
    tm = min(S, 512)
    W = ya.shape[-1]
    row = lambda b, i: (b, i, 0)
    const = lambda b, i: (0, 0)
    gate = lambda n: pl.BlockSpec((None, tm, D), lambda b, i: (b, i, COL_GATES // D + n))
    return pl.pallas_call(
        _merge_kernel,
        out_shape=jax.ShapeDtypeStruct((B, S, D), F32),
        grid=(B, S // tm),
        in_specs=[pl.BlockSpec((None, tm, W), row), pl.BlockSpec((None, tm, W), row),
                  pl.BlockSpec((None, tm, W), row), gate(0), gate(1), gate(2),
                  pl.BlockSpec((None, tm, D), row),
                  pl.BlockSpec((None, 1, D), lambda b, i: (b, 0, 0)),
                  pl.BlockSpec((W, D), const), pl.BlockSpec((W, D), const), pl.BlockSpec((W, D), const),
                  pl.BlockSpec((D, D), const)],
        out_specs=pl.BlockSpec((None, tm, D), row),
        compiler_params=_cparams("parallel", "parallel"),
        name="branch_merge",
    )(ya, yb, yc, proj, proj, proj, x, g1, pa, pb, pc, wo)


def _ffn_kernel(x_ref, sc_ref, sh_ref, g2_ref, wg_ref, wu_ref, wd_ref, o_ref, h_ref, acc_ref):
    f = pl.program_id(2)

    @pl.when(f == 0)
    def _():
        h = _rms(x_ref[...]) * (1.0 + sc_ref[...]) + sh_ref[...]
        h_ref[...] = h.astype(BF16)
        acc_ref[...] = jnp.zeros_like(acc_ref)

    h = h_ref[...]
    a = _silu(jnp.dot(h, wg_ref[...], preferred_element_type=F32)) \
        * jnp.dot(h, wu_ref[...], preferred_element_type=F32)
    acc_ref[...] += jnp.dot(a.astype(BF16), wd_ref[...], preferred_element_type=F32)

    @pl.when(f == pl.num_programs(2) - 1)
    def _():
        o_ref[...] = x_ref[...] + g2_ref[...] * acc_ref[...]


def _dense_ffn(x, sc, sh, g2, wg, wu, wd):
    B, S, D = x.shape
    F = wg.shape[1]
    tm = min(S, 1024)
    tf = 256
    row = lambda b, i, f: (b, i, 0)
    vec = lambda b, i, f: (b, 0, 0)
    return pl.pallas_call(
        _ffn_kernel,
        out_shape=jax.ShapeDtypeStruct((B, S, D), F32),
        grid=(B, S // tm, F // tf),
        in_specs=[pl.BlockSpec((None, tm, D), row),
                  pl.BlockSpec((None, 1, D), vec), pl.BlockSpec((None, 1, D), vec),
                  pl.BlockSpec((None, 1, D), vec),
                  pl.BlockSpec((D, tf), lambda b, i, f: (0, f)),
                  pl.BlockSpec((D, tf), lambda b, i, f: (0, f)),
                  pl.BlockSpec((tf, D), lambda b, i, f: (f, 0))],
        out_specs=pl.BlockSpec((None, tm, D), row),
        scratch_shapes=[pltpu.VMEM((tm, D), BF16), pltpu.VMEM((tm, D), F32)],
        compiler_params=_cparams("parallel", "parallel", "arbitrary"),
        name="dense_ffn",
    )(x, sc, sh, g2, wg, wu, wd)


def _moe_pre_kernel(x_ref, sc_ref, sh_ref, r_ref, h_ref, top_ref):
    h = _rms(x_ref[...]) * (1.0 + sc_ref[...]) + sh_ref[...]
    h_ref[...] = h.astype(BF16)
    logits = _dot_hi(h, r_ref[...])
    lane = lax.broadcasted_iota(jnp.int32, logits.shape, 1)
    lane_f = lane.astype(F32)
    logits = jnp.where(lane < N_EXPERTS, logits, -jnp.inf)
    m1 = jnp.max(logits, axis=-1, keepdims=True)
    i1 = jnp.min(jnp.where(logits == m1, lane_f, float(LANES)), axis=-1, keepdims=True)
    rest = jnp.where(lane_f == i1, -jnp.inf, logits)
    m2 = jnp.max(rest, axis=-1, keepdims=True)
    i2 = jnp.min(jnp.where(rest == m2, lane_f, float(LANES)), axis=-1, keepdims=True)
    e2 = jnp.exp(m2 - m1)
    w1 = 1.0 / (1.0 + e2)
    w2 = e2 / (1.0 + e2)
    top_ref[...] = jnp.where(lane == 0, i1, jnp.where(lane == 1, i2,
                             jnp.where(lane == 2, w1, jnp.where(lane == 3, w2, 0.0))))


def _moe_pre(x, sc, sh, router_pad):
    B, S, D = x.shape
    tm = min(S, 512)
    row = lambda b, i: (b, i, 0)
    vec = lambda b, i: (b, 0, 0)
    return pl.pallas_call(
        _moe_pre_kernel,
        out_shape=(jax.ShapeDtypeStruct((B, S, D), BF16), jax.ShapeDtypeStruct((B, S, LANES), F32)),
        grid=(B, S // tm),
        in_specs=[pl.BlockSpec((None, tm, D), row), pl.BlockSpec((None, 1, D), vec),
                  pl.BlockSpec((None, 1, D), vec), pl.BlockSpec((D, LANES), lambda b, i: (0, 0))],
        out_specs=(pl.BlockSpec((None, tm, D), row), pl.BlockSpec((None, tm, LANES), row)),
        compiler_params=_cparams("parallel", "parallel"),
        name="moe_route",
    )(x, sc, sh, router_pad)


def _moe_dispatch_kernel(dest_ref, h_ref, init_ref, o_ref, sem, *, tm):
    del init_ref
    i, n = pl.program_id(0), pl.num_programs(0)

    def wait(slot):
        pltpu.make_async_copy(h_ref.at[pl.ds(0, TOP_K * tm)], o_ref.at[pl.ds(0, TOP_K * tm)], sem.at[slot]).wait()

    def body(r, carry):
        for k in range(TOP_K):
            pltpu.make_async_copy(h_ref.at[pl.ds(i * tm + r, 1)], o_ref.at[pl.ds(dest_ref[0, 0, TOP_K * r + k], 1)],
                                  sem.at[i % 2]).start()
        return carry

    lax.fori_loop(0, tm, body, 0, unroll=4)

    @pl.when(i > 0)
    def _():
        wait((i + 1) % 2)

    @pl.when(i == n - 1)
    def _():
        wait(i % 2)


def _moe_dispatch(h_rows, dest, slots):
    T = h_rows.shape[0]
    tm = min(T, 512)
    assert TOP_K * tm <= T
    return pl.pallas_call(
        functools.partial(_moe_dispatch_kernel, tm=tm),
        out_shape=jax.ShapeDtypeStruct((slots,) + h_rows.shape[1:], h_rows.dtype),
        grid=(T // tm,),
        in_specs=[pl.BlockSpec((1, 1, TOP_K * tm), lambda i: (i, 0, 0), memory_space=pltpu.SMEM),
                  pl.BlockSpec(memory_space=pl.ANY),
                  pl.BlockSpec(memory_space=pl.ANY)],
        out_specs=pl.BlockSpec(memory_space=pl.ANY),
        scratch_shapes=[pltpu.SemaphoreType.DMA((2,))],
        input_output_aliases={2: 0},
        compiler_params=_cparams("arbitrary"),
        name="moe_dispatch",
    )(dest.reshape(T // tm, 1, TOP_K * tm), h_rows, jnp.zeros((slots,) + h_rows.shape[1:], h_rows.dtype))


def _moe_up_kernel(be_ref, x_ref, wg_ref, wu_ref, o_ref, wg_bf, wu_bf):
    b = pl.program_id(1)
    prev = be_ref[jnp.maximum(b - 1, 0)]

    @pl.when((b == 0) | (be_ref[b] != prev))
    def _():
        wg_bf[...] = wg_ref[...].astype(BF16)
        wu_bf[...] = wu_ref[...].astype(BF16)

    x = x_ref[...]
    a = _silu(jnp.dot(x, wg_bf[...], preferred_element_type=F32)) \
        * jnp.dot(x, wu_bf[...], preferred_element_type=F32)
    o_ref[...] = a.astype(BF16)


def _moe_up(x_slots, block_e, w_gate, w_up, layer):
    slots, D = x_slots.shape
    F = w_gate.shape[-1]
    tf = 896
    n_blocks = slots // MOE_BLOCK
    wspec = pl.BlockSpec((None, None, D, tf), lambda f, b, be: (layer, be[b], 0, f))
    return pl.pallas_call(
        _moe_up_kernel,
        out_shape=jax.ShapeDtypeStruct((slots, F), BF16),
        grid_spec=pltpu.PrefetchScalarGridSpec(
            num_scalar_prefetch=1,
            grid=(F // tf, n_blocks),
            in_specs=[pl.BlockSpec((MOE_BLOCK, D), lambda f, b, be: (b, 0)), wspec, wspec],
            out_specs=pl.BlockSpec((MOE_BLOCK, tf), lambda f, b, be: (b, f)),
            scratch_shapes=[pltpu.VMEM((D, tf), BF16), pltpu.VMEM((D, tf), BF16)]),
        compiler_params=_cparams("arbitrary", "arbitrary"),
        name="moe_gate_up",
    )(block_e, x_slots, w_gate, w_up)


def _moe_down_kernel(be_ref, a_ref, wd_ref, o_ref, wd_bf):
    b = pl.program_id(1)
    prev = be_ref[jnp.maximum(b - 1, 0)]

    @pl.when((b == 0) | (be_ref[b] != prev))
    def _():
        wd_bf[...] = wd_ref[...].astype(BF16)

    o_ref[...] = jnp.dot(a_ref[...], wd_bf[...], preferred_element_type=F32)


def _moe_down(a_slots, block_e, w_down, layer):
    slots, F = a_slots.shape
    D = w_down.shape[-1]
    tn = 512
    n_blocks = slots // MOE_BLOCK
    return pl.pallas_call(
        _moe_down_kernel,
        out_shape=jax.ShapeDtypeStruct((slots, D), F32),
        grid_spec=pltpu.PrefetchScalarGridSpec(
            num_scalar_prefetch=1,
            grid=(D // tn, n_blocks),
            in_specs=[pl.BlockSpec((MOE_BLOCK, F), lambda n, b, be: (b, 0)),
                      pl.BlockSpec((None, None, F, tn), lambda n, b, be: (layer, be[b], 0, n))],
            out_specs=pl.BlockSpec((MOE_BLOCK, tn), lambda n, b, be: (b, n)),
            scratch_shapes=[pltpu.VMEM((F, tn), BF16)]),
        compiler_params=_cparams("arbitrary", "arbitrary"),
        name="moe_down",
    )(block_e, a_slots, w_down)


def _moe_combine_kernel(cur_ref, nxt_ref, x_ref, g2_ref, top_ref, y_ref, o_ref, buf_ref, sem, *, tm):
    i = pl.program_id(0) * pl.num_programs(1) + pl.program_id(1)
    n = pl.num_programs(0) * pl.num_programs(1)

    def issue(idx_ref, slot):
        def body(r, carry):
            for k in range(TOP_K):
                pltpu.make_async_copy(y_ref.at[pl.ds(idx_ref[0, 0, TOP_K * r + k], 1)],
                                      buf_ref.at[slot, k, pl.ds(r, 1)], sem.at[slot]).start()
            return carry
        lax.fori_loop(0, tm, body, 0, unroll=4)

    @pl.when(i == 0)
    def _():
        issue(cur_ref, 0)

    @pl.when(i + 1 < n)
    def _():
        issue(nxt_ref, (i + 1) % 2)

    slot = i % 2
    for k in range(TOP_K):
        pltpu.make_async_copy(y_ref.at[pl.ds(0, tm)], buf_ref.at[slot, k], sem.at[slot]).wait()
    top = top_ref[...]
    f = top[:, 2:3] * buf_ref[slot, 0] + top[:, 3:4] * buf_ref[slot, 1]
    o_ref[...] = x_ref[...] + g2_ref[...] * f


def _moe_combine(x, g2, top, dest, y_slots):
    B, S, D = x.shape
    tm = min(S, 256)
    nb = S // tm
    row = lambda b, i: (b, i, 0)
    idx = dest.reshape(B * nb, 1, TOP_K * tm)
    return pl.pallas_call(
        functools.partial(_moe_combine_kernel, tm=tm),
        out_shape=jax.ShapeDtypeStruct((B, S, D), F32),
        grid=(B, nb),
        in_specs=[pl.BlockSpec((1, 1, TOP_K * tm), lambda b, i: (b * nb + i, 0, 0), memory_space=pltpu.SMEM),
                  pl.BlockSpec((1, 1, TOP_K * tm), lambda b, i: (jnp.minimum(b * nb + i + 1, B * nb - 1), 0, 0),
                               memory_space=pltpu.SMEM),
                  pl.BlockSpec((None, tm, D), row),
                  pl.BlockSpec((None, 1, D), lambda b, i: (b, 0, 0)),
                  pl.BlockSpec((None, tm, LANES), row),
                  pl.BlockSpec(memory_space=pl.ANY)],
        out_specs=pl.BlockSpec((None, tm, D), row),
        scratch_shapes=[pltpu.VMEM((2, TOP_K, tm, D), F32), pltpu.SemaphoreType.DMA((2,))],
        compiler_params=_cparams("arbitrary", "arbitrary"),
        name="moe_combine",
    )(idx, idx, x, g2, top, y_slots)


def _moe_ffn(x, sc, sh, g2, router, w_gate, w_up, w_down, layer):
    B, S, D = x.shape
    T = B * S
    router_pad = jnp.zeros((D, LANES), F32).at[:, :N_EXPERTS].set(router)
    h, top = _moe_pre(x, sc, sh, router_pad)
    top2d = top.reshape(T, LANES)
    flat_e = top2d[:, :TOP_K].astype(jnp.int32).reshape(-1)
    n_assign = T * TOP_K
    n_blocks = (n_assign + N_EXPERTS * (MOE_BLOCK - 1)) // MOE_BLOCK
    slots = n_blocks * MOE_BLOCK
    onehot = (flat_e[:, None] == jnp.arange(N_EXPERTS, dtype=jnp.int32)[None, :]).astype(jnp.int32)
    csum = jnp.cumsum(onehot, axis=0)
    rank = jnp.sum(csum * onehot, axis=1) - 1
    counts = csum[-1]
    padded = (counts + MOE_BLOCK - 1) // MOE_BLOCK * MOE_BLOCK
    pad_end = jnp.cumsum(padded)
    start_padded = pad_end - padded
    dest = (start_padded[flat_e] + rank).astype(jnp.int32)
    block_start = jnp.arange(n_blocks, dtype=jnp.int32) * MOE_BLOCK
    block_e = jnp.minimum(jnp.sum((block_start[:, None] >= pad_end[None, :]).astype(jnp.int32), axis=1),
                          N_EXPERTS - 1).astype(jnp.int32)
    x_slots = _moe_dispatch(h.reshape(T, D // LANES, LANES), dest, slots).reshape(slots, D)
    a_slots = _moe_up(x_slots, block_e, w_gate, w_up, layer)
    y_slots = _moe_down(a_slots, block_e, w_down, layer)
    return _moe_combine(x, g2, top, dest, y_slots)


def _final_norm_kernel(x_ref, w_ref, o_ref):
    o_ref[...] = _rms(x_ref[...]) * w_ref[...]


def _final_norm(x2d, w):
    T, D = x2d.shape
    tm = min(T, 1024)
    return pl.pallas_call(
        _final_norm_kernel,
        out_shape=jax.ShapeDtypeStruct((T, D), F32),
        grid=(T // tm,),
        in_specs=[pl.BlockSpec((tm, D), lambda i: (i, 0)), pl.BlockSpec((1, D), lambda i: (0, 0))],
        out_specs=pl.BlockSpec((tm, D), lambda i: (i, 0)),
        compiler_params=_cparams("parallel"),
        name="final_norm",
    )(x2d, w.reshape(1, D))


def _arrange_w_in(w):
    D = w.shape[0]
    o = np.cumsum([0, 3 * GDN_DIM, GDN_DIM, GDN_HEADS, GDN_HEADS, SSM_INNER, SSM_CONV_DIM, SSM_HEADS,
                   MLA_Q_LORA, MLA_KV_LORA, MLA_ROPE, 3 * w.shape[0]])
    seg = lambda i: w[:, int(o[i]):int(o[i + 1])]
    qkv, gz, ga, gb, sz, xbc, dt, cq, ckv, kr, gates = [seg(i) for i in range(11)]
    half = MLA_ROPE // 2
    small = jnp.concatenate([ga, gb, dt, jnp.zeros((D, LANES - 2 * GDN_HEADS - SSM_HEADS), w.dtype)], axis=1)
    kr_full = jnp.concatenate([kr, -kr[:, half:], kr[:, :half]], axis=1)
    out = jnp.concatenate([qkv, gz, sz, cq, xbc, gates, ckv, small, kr_full], axis=1)
    assert out.shape[1] == IN_COLS
    return out.astype(BF16)


def _arrange_w_uq(w):
    half = MLA_ROPE // 2
    parts = []
    for h in range(MLA_HEADS):
        c0 = h * (MLA_NOPE + MLA_ROPE)
        rope = w[:, c0 + MLA_NOPE:c0 + MLA_NOPE + MLA_ROPE]
        parts += [w[:, c0:c0 + MLA_NOPE], rope, -rope[:, half:], rope[:, :half]]
    return jnp.concatenate(parts, axis=1).astype(BF16)


def kernel(x, c, positions, w_ada, b_ada, w_in, gdn_conv_w, gdn_a_log, gdn_dt_bias, gdn_norm_w, ssm_conv_w, ssm_conv_b, ssm_a_log, ssm_dt_bias, ssm_d, ssm_norm_w, mla_q_norm_w, mla_w_uq, mla_kv_norm_w, mla_w_uk, mla_w_uv, w_branch_a, w_branch_b, w_branch_c, w_out, ffn_w_gate, ffn_w_up, ffn_w_down, moe_router, moe_w_gate, moe_w_up, moe_w_down, final_norm_w):
    B, S, D = x.shape
    T = B * S
    depth = w_ada.shape[0]
    c_pad = jnp.zeros((SUBLANES, D), F32).at[:B].set(c)
    mod = _modulation(c_pad, w_ada.astype(BF16), b_ada)[:, :B].reshape(depth, B, 6, 1, D)
    cos, sin = _rope_tables(positions)
    for l in range(depth):
        sh1, sc1, g1, sh2, sc2, g2 = [mod[l, :, n] for n in range(6)]
        proj = _in_proj(x, sc1, sh1, _arrange_w_in(w_in[l]))
        y_a = _gdn(proj, gdn_conv_w[l], gdn_a_log[l], gdn_dt_bias[l], gdn_norm_w[l])
        y_b = _ssd(proj, ssm_conv_w[l], ssm_conv_b[l], ssm_a_log[l], ssm_dt_bias[l], ssm_d[l], ssm_norm_w[l])
        q, k, v = _mla_proj(proj.reshape(T, IN_COLS), cos, sin, mla_q_norm_w[l], _arrange_w_uq(mla_w_uq[l]),
                            mla_kv_norm_w[l], mla_w_uk[l].astype(BF16), mla_w_uv[l].astype(BF16))
        y_c = _attention(q.reshape(B, S, -1), k.reshape(B, S, -1), v.reshape(B, S, -1))
        x = _merge(y_a, y_b, y_c, proj, x, g1, w_branch_a[l].astype(BF16), w_branch_b[l].astype(BF16),
                   w_branch_c[l].astype(BF16), w_out[l].astype(BF16))
        i = l // 2
        if l % 2 == 0:
            x = _dense_ffn(x, sc2, sh2, g2, ffn_w_gate[i].astype(BF16), ffn_w_up[i].astype(BF16),
                           ffn_w_down[i].astype(BF16))
        else:
            x = _moe_ffn(x, sc2, sh2, g2, moe_router[i], moe_w_gate, moe_w_up, moe_w_down, i)
    return _final_norm(x.reshape(T, D), final_norm_w).reshape(B, S, D)
```

```python
import functools
import math

import numpy as np
import jax
import jax.numpy as jnp
from jax import lax
from jax.experimental import pallas as pl
from jax.experimental.pallas import tpu as pltpu

F32 = jnp.float32
BF16 = jnp.bfloat16
HI = lax.Precision.HIGHEST

EPS = 1e-6
CHUNK = 64
CONV_K = 4
GDN_HEADS = 4
GDN_HEAD_DIM = 128
GDN_DIM = GDN_HEADS * GDN_HEAD_DIM
SSM_HEADS = 8
SSM_HEAD_DIM = 64
SSM_INNER = SSM_HEADS * SSM_HEAD_DIM
SSM_GROUPS = 2
SSM_STATE = 128
SSM_CONV_DIM = SSM_INNER + 2 * SSM_GROUPS * SSM_STATE
MLA_HEADS = 4
MLA_Q_LORA = 512
MLA_KV_LORA = 256
MLA_NOPE = 128
MLA_ROPE = 64
MLA_V = 128
MLA_DIM = MLA_HEADS * MLA_V
ROPE_THETA = 10000.0
N_EXPERTS = 8
TOP_K = 2
MOE_BLOCK = 1024

LANES = 128
SUBLANES = 8
VMEM_LIMIT = 56 * 1024 * 1024

COL_QKV = 0
COL_GZ = 1536
COL_SZ = 2048
COL_CQ = 2560
COL_XBC = 3072
COL_GATES = 4096
COL_CKV = 7168
COL_SMALL = 7424
COL_KR = 7552
IN_COLS = 7680


def _cparams(*sem):
    return pltpu.CompilerParams(dimension_semantics=tuple(sem), vmem_limit_bytes=VMEM_LIMIT)


def _silu(x):
    return x * (1.0 / (1.0 + jnp.exp(-x)))


def _sigmoid(x):
    return 1.0 / (1.0 + jnp.exp(-x))


def _softplus(x):
    return jnp.maximum(x, 0.0) + jnp.log(1.0 + jnp.exp(-jnp.abs(x)))


def _rms(x):
    return x * lax.rsqrt(jnp.mean(x * x, axis=-1, keepdims=True) + EPS)


def _dot(a, b):
    return jnp.dot(a.astype(BF16), b.astype(BF16), preferred_element_type=F32)


def _dot_nt(a, b):
    return lax.dot_general(a.astype(BF16), b.astype(BF16), (((1,), (1,)), ((), ())),
                           preferred_element_type=F32)


def _dot_tn(a, b):
    return lax.dot_general(a.astype(BF16), b.astype(BF16), (((0,), (0,)), ((), ())),
                           preferred_element_type=F32)


def _dot_hi(a, b):
    return jnp.dot(a, b, precision=HI, preferred_element_type=F32)


def _mod_kernel(c_ref, w_ref, b_ref, o_ref):
    cond = _silu(c_ref[...])
    o_ref[...] = _dot(cond, w_ref[...]) + b_ref[...]


def _modulation(c_pad, w_ada, b_ada):
    L, D, N = w_ada.shape
    tn = 1536
    return pl.pallas_call(
        _mod_kernel,
        out_shape=jax.ShapeDtypeStruct((L, SUBLANES, N), F32),
        grid=(L, N // tn),
        in_specs=[pl.BlockSpec((SUBLANES, D), lambda l, j: (0, 0)),
                  pl.BlockSpec((None, D, tn), lambda l, j: (l, 0, j)),
                  pl.BlockSpec((None, 1, tn), lambda l, j: (l, 0, j))],
        out_specs=pl.BlockSpec((None, SUBLANES, tn), lambda l, j: (l, 0, j)),
        compiler_params=_cparams("parallel", "parallel"),
        name="adaln_mod",
    )(c_pad, w_ada, b_ada.reshape(L, 1, N))


def _rope_table_kernel(pos_ref, freq_ref, cos_ref, sin_ref):
    ang = pos_ref[...].astype(F32) * freq_ref[...]
    keep = lax.broadcasted_iota(jnp.int32, ang.shape, 1) < MLA_ROPE
    cos_ref[...] = jnp.where(keep, jnp.cos(ang), 0.0)
    sin_ref[...] = jnp.where(keep, jnp.sin(ang), 0.0)


def _rope_tables(positions):
    T = positions.size
    tm = min(T, 1024)
    inv_freq = ROPE_THETA ** (-jnp.arange(0, MLA_ROPE, 2, dtype=F32) / MLA_ROPE)
    freq = jnp.concatenate([inv_freq, inv_freq, jnp.zeros((LANES - MLA_ROPE,), F32)]).reshape(1, LANES)
    return pl.pallas_call(
        _rope_table_kernel,
        out_shape=(jax.ShapeDtypeStruct((T, LANES), F32), jax.ShapeDtypeStruct((T, LANES), F32)),
        grid=(T // tm,),
        in_specs=[pl.BlockSpec((tm, 1), lambda i: (i, 0)),
                  pl.BlockSpec((1, LANES), lambda i: (0, 0))],
        out_specs=(pl.BlockSpec((tm, LANES), lambda i: (i, 0)),
                   pl.BlockSpec((tm, LANES), lambda i: (i, 0))),
        compiler_params=_cparams("parallel"),
        name="rope_tables",
    )(positions.reshape(T, 1), freq)


IN_TN = 1536
SMALL_TILE = COL_SMALL // IN_TN
SMALL_OFF = COL_SMALL % IN_TN


def _in_proj_kernel(x_ref, sc_ref, sh_ref, w_ref, o_ref, small_ref, h_ref):
    j = pl.program_id(2)

    @pl.when(j == 0)
    def _():
        h = _rms(x_ref[...]) * (1.0 + sc_ref[...]) + sh_ref[...]
        h_ref[...] = h.astype(BF16)

    o = jnp.dot(h_ref[...], w_ref[...], preferred_element_type=F32)
    o_ref[...] = o.astype(BF16)

    @pl.when(j == SMALL_TILE)
    def _():
        small_ref[...] = o[:, SMALL_OFF:SMALL_OFF + LANES]


def _in_proj(x, sc, sh, w):
    B, S, D = x.shape
    N = w.shape[1]
    tm = min(S, 1024)
    tn = IN_TN
    return pl.pallas_call(
        _in_proj_kernel,
        out_shape=(jax.ShapeDtypeStruct((B, S, N), BF16), jax.ShapeDtypeStruct((B, S, LANES), F32)),
        grid=(B, S // tm, N // tn),
        in_specs=[pl.BlockSpec((None, tm, D), lambda b, i, j: (b, i, 0)),
                  pl.BlockSpec((None, 1, D), lambda b, i, j: (b, 0, 0)),
                  pl.BlockSpec((None, 1, D), lambda b, i, j: (b, 0, 0)),
                  pl.BlockSpec((D, tn), lambda b, i, j: (0, j))],
        out_specs=(pl.BlockSpec((None, tm, tn), lambda b, i, j: (b, i, j)),
                   pl.BlockSpec((None, tm, LANES), lambda b, i, j: (b, i, 0))),
        scratch_shapes=[pltpu.VMEM((tm, D), BF16)],
        compiler_params=_cparams("parallel", "parallel", "arbitrary"),
        name="in_proj",
    )(x, sc, sh, w)


def _causal_conv(x, tail_ref, w):
    rows = x.shape[0]
    xe = jnp.concatenate([tail_ref[...], x], axis=0)
    tail_ref[...] = x[rows - SUBLANES:, :]
    y = xe[SUBLANES:, :] * w[CONV_K - 1:CONV_K, :]
    for j in range(CONV_K - 1):
        shift = CONV_K - 1 - j
        y = y + pltpu.roll(xe, shift, axis=0)[SUBLANES:, :] * w[j:j + 1, :]
    return y


def _chunk_cumsum(v, chunk=CHUNK):
    pos = lax.broadcasted_iota(jnp.int32, v.shape, 0) % chunk
    step = 1
    while step < chunk:
        v = v + jnp.where(pos >= step, pltpu.roll(v, step, axis=0), 0.0)
        step *= 2
    return v


def _col_to_row(col_b, eye):
    return jnp.sum(jnp.where(eye, col_b, 0.0), axis=0, keepdims=True)


GDN_CHUNK = 128


def _split(a):
    return a.astype(BF16)


def _mm_split(a, b):
    return jnp.dot(a, b, preferred_element_type=F32)


def _gdn_kernel(qkv_ref, z_ref, sm_ref, cw_ref, par_ref, nw_ref, o_ref,
                tail_ref, state_ref, *, rows):
    C, H, Dh = GDN_CHUNK, GDN_HEADS, GDN_HEAD_DIM
    n_ck = rows // C

    @pl.when(pl.program_id(1) == 0)
    def _():
        tail_ref[...] = jnp.zeros_like(tail_ref)
        state_ref[...] = jnp.zeros_like(state_ref)

    y = _silu(_causal_conv(qkv_ref[...].astype(F32), tail_ref, cw_ref[...]))
    sm = sm_ref[...]
    neg_a = par_ref[0:1, :]
    dt_b = par_ref[1:2, :]
    g_all = _chunk_cumsum(neg_a * _softplus(sm + dt_b), C)
    beta_all = _sigmoid(sm)
    z = z_ref[...].astype(F32)
    nw = nw_ref[...]

    ri = lax.broadcasted_iota(jnp.int32, (C, C), 0)
    ci = lax.broadcasted_iota(jnp.int32, (C, C), 1)
    eye = ri == ci
    incl = ri >= ci
    strict = ri > ci
    eye_f = jnp.where(eye, 1.0, 0.0)

    qn, kn, vn = [], [], []
    for h in range(H):
        q_h = y[:, h * Dh:(h + 1) * Dh]
        k_h = y[:, GDN_DIM + h * Dh:GDN_DIM + (h + 1) * Dh]
        qn.append(q_h * lax.rsqrt(jnp.sum(q_h * q_h, axis=-1, keepdims=True) + EPS) * (Dh ** -0.5))
        kn.append(k_h * lax.rsqrt(jnp.sum(k_h * k_h, axis=-1, keepdims=True) + EPS))
        vn.append(y[:, 2 * GDN_DIM + h * Dh:2 * GDN_DIM + (h + 1) * Dh])

    items = [(c, h) for c in range(n_ck) for h in range(H)]
    p, rhs, qk, q_dec, k_dec, g_end = {}, {}, {}, {}, {}, {}
    for c, h in items:
        r0 = c * C
        q_c, k_c, v_c = qn[h][r0:r0 + C], kn[h][r0:r0 + C], vn[h][r0:r0 + C]
        g_col = g_all[r0:r0 + C, h:h + 1]
        beta = beta_all[r0:r0 + C, H + h:H + h + 1]
        g_b = jnp.broadcast_to(g_col, (C, C))
        g_last = g_col[C - 1:C, :]
        decay = jnp.exp(jnp.where(incl, g_b - _col_to_row(g_b, eye), -1e30))
        eg = jnp.exp(g_col)
        p[c, h] = jnp.where(strict, -(beta * _dot_nt(k_c, k_c) * decay), 0.0)
        qk[c, h] = (_dot_nt(q_c, k_c) * decay).astype(BF16)
        rhs[c, h] = _split(jnp.concatenate([v_c * beta, k_c * (beta * eg)], axis=-1))
        q_dec[c, h] = (q_c * eg).astype(BF16)
        k_dec[c, h] = (k_c * jnp.exp(g_last - g_col)).astype(BF16)
        g_end[c, h] = jnp.exp(g_last)

    inv = {it: eye_f + jnp.where((ri // 2 == ci // 2) & strict, p[it], 0.0) for it in items}
    s = 2
    while s < C:
        level = ((ri // s) % 2 == 1) & ((ci // s) % 2 == 0) & (ri // (2 * s) == ci // (2 * s))
        t_split = {it: _split(inv[it]) for it in items}
        x = {it: _mm_split(_split(jnp.where(level, p[it], 0.0)), t_split[it]) for it in items}
        inv = {it: inv[it] + _mm_split(t_split[it], _split(x[it])) for it in items}
        s *= 2
    sol = {it: _mm_split(_split(inv[it]), rhs[it]) for it in items}

    states = [state_ref[h] for h in range(H)]
    for c in range(n_ck):
        r0 = c * C
        s_bf = [s.astype(BF16) for s in states]
        v_new = [sol[c, h][:, :Dh] - _dot(sol[c, h][:, Dh:], s_bf[h]) for h in range(H)]
        v_bf = [v.astype(BF16) for v in v_new]
        outs = [jnp.dot(q_dec[c, h], s_bf[h], preferred_element_type=F32)
                + jnp.dot(qk[c, h], v_bf[h], preferred_element_type=F32) for h in range(H)]
        states = [states[h] * g_end[c, h]
                  + lax.dot_general(k_dec[c, h], v_bf[h], (((0,), (0,)), ((), ())), preferred_element_type=F32)
                  for h in range(H)]
        for h in range(H):
            z_c = z[r0:r0 + C, h * Dh:(h + 1) * Dh]
            o_ref[r0:r0 + C, h * Dh:(h + 1) * Dh] = (_rms(outs[h]) * nw * _silu(z_c)).astype(BF16)
    for h in range(H):
        state_ref[h] = states[h]


def _gdn(proj, small, conv_w, a_log, dt_bias, norm_w):
    B, S, _ = proj.shape
    rows = min(S, 512)
    par = jnp.zeros((SUBLANES, LANES), F32)
    par = par.at[0, :GDN_HEADS].set(-jnp.exp(a_log)).at[1, :GDN_HEADS].set(dt_bias)
    w3 = 3 * GDN_DIM
    return pl.pallas_call(
        functools.partial(_gdn_kernel, rows=rows),
        out_shape=jax.ShapeDtypeStruct((B, S, GDN_DIM), BF16),
        grid=(B, S // rows),
        in_specs=[pl.BlockSpec((None, rows, w3), lambda b, i: (b, i, COL_QKV // w3)),
                  pl.BlockSpec((None, rows, GDN_DIM), lambda b, i: (b, i, COL_GZ // GDN_DIM)),
                  pl.BlockSpec((None, rows, LANES), lambda b, i: (b, i, 0)),
                  pl.BlockSpec((CONV_K, w3), lambda b, i: (0, 0)),
                  pl.BlockSpec((SUBLANES, LANES), lambda b, i: (0, 0)),
                  pl.BlockSpec((1, GDN_HEAD_DIM), lambda b, i: (0, 0))],
        out_specs=pl.BlockSpec((None, rows, GDN_DIM), lambda b, i: (b, i, 0)),
        scratch_shapes=[pltpu.VMEM((SUBLANES, w3), F32),
                        pltpu.VMEM((GDN_HEADS, GDN_HEAD_DIM, GDN_HEAD_DIM), F32)],
        compiler_params=_cparams("parallel", "arbitrary"),
        name="gated_delta_net",
    )(proj, proj, small, conv_w, par, norm_w.reshape(1, GDN_HEAD_DIM))


SSD_CHUNK = 128


def _ssd_kernel(xbc_ref, z_ref, sm_ref, cw_ref, cb_ref, par_ref, nw_ref, o_ref,
                tail_ref, state_ref, *, rows):
    C, H, P, G, N = SSD_CHUNK, SSM_HEADS, SSM_HEAD_DIM, SSM_GROUPS, SSM_STATE
    E = H // G
    GW = E * P

    @pl.when(pl.program_id(1) == 0)
    def _():
        tail_ref[...] = jnp.zeros_like(tail_ref)
        state_ref[...] = jnp.zeros_like(state_ref)

    y = _silu(_causal_conv(xbc_ref[...].astype(F32), tail_ref, cw_ref[...]) + cb_ref[...])
    sm = sm_ref[...]
    neg_a = par_ref[0:1, :]
    dt_b = par_ref[1:2, :]
    d_skip = par_ref[2:3, :]
    dt_all = _softplus(sm + dt_b)
    acs_all = _chunk_cumsum(dt_all * neg_a, C)
    z = z_ref[...].astype(F32)
    nw = nw_ref[...]

    ri = lax.broadcasted_iota(jnp.int32, (C, C), 0)
    ci = lax.broadcasted_iota(jnp.int32, (C, C), 1)
    eye = ri == ci
    incl = ri >= ci

    items = [(c, g) for c in range(rows // C) for g in range(G)]
    cm, y_local, st_add, e_in, e_out = {}, {}, {}, {}, {}
    for c, g in items:
        r0 = c * C
        bm_c = y[r0:r0 + C, SSM_INNER + g * N:SSM_INNER + (g + 1) * N]
        cm_c = y[r0:r0 + C, SSM_INNER + G * N + g * N:SSM_INNER + G * N + (g + 1) * N]
        x_c = y[r0:r0 + C, g * GW:(g + 1) * GW]
        cb = _dot_nt(cm_c, bm_c)
        xs_parts, ea_parts, ydiag_parts, dx_parts, last_parts = [], [], [], [], []
        for e in range(E):
            lane = SUBLANES + g * E + e
            a_col = acs_all[r0:r0 + C, lane:lane + 1]
            dt_col = dt_all[r0:r0 + C, lane:lane + 1]
            a_b = jnp.broadcast_to(a_col, (C, C))
            a_last = a_col[C - 1:C, :]
            L = jnp.exp(jnp.where(incl, a_b - _col_to_row(a_b, eye), -1e30))
            x_e = x_c[:, e * P:(e + 1) * P]
            xd = x_e * dt_col
            ydiag_parts.append(_dot(cb * L, xd))
            xs_parts.append(xd * jnp.exp(a_last - a_col))
            ea_parts.append(jnp.broadcast_to(jnp.exp(a_col), (C, P)))
            last_parts.append(jnp.broadcast_to(jnp.exp(a_last), (1, P)))
            dx_parts.append(x_e * d_skip[:, lane:lane + 1])
        cm[c, g] = cm_c.astype(BF16)
        y_local[c, g] = jnp.concatenate(ydiag_parts, axis=-1) + jnp.concatenate(dx_parts, axis=-1)
        st_add[c, g] = _dot_tn(bm_c, jnp.concatenate(xs_parts, axis=-1))
        e_in[c, g] = jnp.concatenate(ea_parts, axis=-1)
        e_out[c, g] = jnp.concatenate(last_parts, axis=-1)

    states = [state_ref[g] for g in range(G)]
    for c in range(rows // C):
        r0 = c * C
        for g in range(G):
            y_off = jnp.dot(cm[c, g], states[g].astype(BF16), preferred_element_type=F32)
            y_c = (y_local[c, g] + y_off * e_in[c, g]) * _silu(z[r0:r0 + C, g * GW:(g + 1) * GW])
            o_ref[r0:r0 + C, g * GW:(g + 1) * GW] = (_rms(y_c) * nw[:, g * GW:(g + 1) * GW]).astype(BF16)
            states[g] = states[g] * e_out[c, g] + st_add[c, g]
    for g in range(G):
        state_ref[g] = states[g]


def _ssd(proj, small, conv_w, conv_b, a_log, dt_bias, d_skip, norm_w):
    B, S, _ = proj.shape
    rows = min(S, 512)
    par = jnp.zeros((SUBLANES, LANES), F32)
    sl = slice(SUBLANES, SUBLANES + SSM_HEADS)
    par = par.at[0, sl].set(-jnp.exp(a_log)).at[1, sl].set(dt_bias).at[2, sl].set(d_skip)
    return pl.pallas_call(
        functools.partial(_ssd_kernel, rows=rows),
        out_shape=jax.ShapeDtypeStruct((B, S, SSM_INNER), BF16),
        grid=(B, S // rows),
        in_specs=[pl.BlockSpec((None, rows, SSM_CONV_DIM), lambda b, i: (b, i, COL_XBC // SSM_CONV_DIM)),
                  pl.BlockSpec((None, rows, SSM_INNER), lambda b, i: (b, i, COL_SZ // SSM_INNER)),
                  pl.BlockSpec((None, rows, LANES), lambda b, i: (b, i, 0)),
                  pl.BlockSpec((CONV_K, SSM_CONV_DIM), lambda b, i: (0, 0)),
                  pl.BlockSpec((1, SSM_CONV_DIM), lambda b, i: (0, 0)),
                  pl.BlockSpec((SUBLANES, LANES), lambda b, i: (0, 0)),
                  pl.BlockSpec((1, SSM_INNER), lambda b, i: (0, 0))],
        out_specs=pl.BlockSpec((None, rows, SSM_INNER), lambda b, i: (b, i, 0)),
        scratch_shapes=[pltpu.VMEM((SUBLANES, SSM_CONV_DIM), F32),
                        pltpu.VMEM((SSM_GROUPS, SSM_STATE, SSM_INNER // SSM_GROUPS), F32)],
        compiler_params=_cparams("parallel", "arbitrary"),
        name="mamba2_ssd",
    )(proj, proj, small, conv_w, conv_b.reshape(1, -1), par, norm_w.reshape(1, -1))


QK_W = 2 * LANES


def _rope(x, cos, sin):
    return x * cos + pltpu.roll(x, MLA_ROPE, axis=1) * sin


def _mla_proj_kernel(cq_ref, ckv_ref, kr_ref, cos_ref, sin_ref, qw_ref, wq_ref, kvw_ref, wk_ref, wv_ref,
                     q_ref, k_ref, v_ref):
    scale = (MLA_NOPE + MLA_ROPE) ** -0.5
    cos, sin = cos_ref[...], sin_ref[...]
    qc = _rms(cq_ref[...].astype(F32)) * qw_ref[...]
    qf = _dot(qc, wq_ref[...]) * scale
    ckv = (_rms(ckv_ref[...].astype(F32)) * kvw_ref[...]).astype(BF16)
    kn = jnp.dot(ckv, wk_ref[...], preferred_element_type=F32)
    v_ref[...] = jnp.dot(ckv, wv_ref[...], preferred_element_type=F32).astype(BF16)
    kr = _rope(kr_ref[...].astype(F32), cos, sin).astype(BF16)
    for h in range(MLA_HEADS):
        c0 = h * QK_W
        q_ref[:, c0:c0 + LANES] = qf[:, c0:c0 + LANES].astype(BF16)
        q_ref[:, c0 + LANES:c0 + QK_W] = _rope(qf[:, c0 + LANES:c0 + QK_W], cos, sin).astype(BF16)
        k_ref[:, c0:c0 + LANES] = kn[:, h * MLA_NOPE:(h + 1) * MLA_NOPE].astype(BF16)
        k_ref[:, c0 + LANES:c0 + QK_W] = kr


def _mla_proj(proj2d, cos, sin, q_norm_w, wq, kv_norm_w, wk, wv):
    T = proj2d.shape[0]
    tm = min(T, 512)
    HQ = MLA_HEADS * QK_W
    row = lambda i: (i, 0)
    const = lambda i: (0, 0)
    return pl.pallas_call(
        _mla_proj_kernel,
        out_shape=(jax.ShapeDtypeStruct((T, HQ), BF16), jax.ShapeDtypeStruct((T, HQ), BF16),
                   jax.ShapeDtypeStruct((T, MLA_DIM), BF16)),
        grid=(T // tm,),
        in_specs=[pl.BlockSpec((tm, MLA_Q_LORA), lambda i: (i, COL_CQ // MLA_Q_LORA)),
                  pl.BlockSpec((tm, MLA_KV_LORA), lambda i: (i, COL_CKV // MLA_KV_LORA)),
                  pl.BlockSpec((tm, LANES), lambda i: (i, COL_KR // LANES)),
                  pl.BlockSpec((tm, LANES), row),
                  pl.BlockSpec((tm, LANES), row),
                  pl.BlockSpec((1, MLA_Q_LORA), const),
                  pl.BlockSpec((MLA_Q_LORA, HQ), const),
                  pl.BlockSpec((1, MLA_KV_LORA), const),
                  pl.BlockSpec((MLA_KV_LORA, MLA_HEADS * MLA_NOPE), const),
                  pl.BlockSpec((MLA_KV_LORA, MLA_DIM), const)],
        out_specs=(pl.BlockSpec((tm, HQ), row), pl.BlockSpec((tm, HQ), row),
                   pl.BlockSpec((tm, MLA_DIM), row)),
        compiler_params=_cparams("parallel"),
        name="mla_proj",
    )(proj2d, proj2d, proj2d, cos, sin, q_norm_w.reshape(1, -1), wq, kv_norm_w.reshape(1, -1), wk, wv)


def _attn_kernel(q_ref, k_ref, v_ref, o_ref, m_ref, l_ref, acc_ref, *, t):
    qi = pl.program_id(1)
    m_ref[...] = jnp.full_like(m_ref, -1e30)
    l_ref[...] = jnp.zeros_like(l_ref)
    acc_ref[...] = jnp.zeros_like(acc_ref)

    def block(j, diagonal):
        k0 = pl.multiple_of(j * t, t)
        hs = range(MLA_HEADS)
        s = [lax.dot_general(q_ref[:, h * QK_W:(h + 1) * QK_W], k_ref[pl.ds(k0, t), h * QK_W:(h + 1) * QK_W],
                             (((1,), (1,)), ((), ())), preferred_element_type=F32) for h in hs]
        if diagonal:
            row = lax.broadcasted_iota(jnp.int32, s[0].shape, 0)
            col = lax.broadcasted_iota(jnp.int32, s[0].shape, 1)
            s = [jnp.where(row >= col, s[h], -1e30) for h in hs]
        cols = range(t // LANES)
        sc = [[s[h][:, c * LANES:(c + 1) * LANES] for c in cols] for h in hs]
        m_prev = [m_ref[h] for h in hs]
        m_new = [jnp.maximum(m_prev[h], jnp.max(functools.reduce(jnp.maximum, sc[h]), axis=-1, keepdims=True))
                 for h in hs]
        p = [[jnp.exp(sc[h][c] - m_new[h]) for c in cols] for h in hs]
        alpha = [jnp.exp(m_prev[h] - m_new[h]) for h in hs]
        pv = [jnp.dot(jnp.concatenate(p[h], axis=-1).astype(BF16), v_ref[pl.ds(k0, t), h * MLA_V:(h + 1) * MLA_V],
                      preferred_element_type=F32) for h in hs]
        for h in hs:
            l_ref[h] = alpha[h] * l_ref[h] + functools.reduce(jnp.add, p[h])
            acc_ref[h] = alpha[h] * acc_ref[h] + pv[h]
            m_ref[h] = m_new[h]

    def body(j, carry):
        block(j, False)
        return carry

    lax.fori_loop(0, qi, body, 0)
    block(qi, True)
    for h in range(MLA_HEADS):
        o_ref[:, h * MLA_V:(h + 1) * MLA_V] = (acc_ref[h] / jnp.sum(l_ref[h], axis=-1, keepdims=True)).astype(BF16)


def _attention(q, k, v):
    B, S, _ = q.shape
    t = min(S, 512)
    H = MLA_HEADS
    return pl.pallas_call(
        functools.partial(_attn_kernel, t=t),
        out_shape=jax.ShapeDtypeStruct((B, S, MLA_DIM), BF16),
        grid=(B, S // t),
        in_specs=[pl.BlockSpec((None, t, H * QK_W), lambda b, qi: (b, qi, 0)),
                  pl.BlockSpec((None, S, H * QK_W), lambda b, qi: (b, 0, 0)),
                  pl.BlockSpec((None, S, MLA_DIM), lambda b, qi: (b, 0, 0))],
        out_specs=pl.BlockSpec((None, t, MLA_DIM), lambda b, qi: (b, qi, 0)),
        scratch_shapes=[pltpu.VMEM((H, t, LANES), F32), pltpu.VMEM((H, t, LANES), F32),
                        pltpu.VMEM((H, t, MLA_V), F32)],
        compiler_params=_cparams("parallel", "arbitrary"),
        name="mla_attention",
    )(q, k, v)


def _merge_kernel(ya_ref, yb_ref, yc_ref, ga_ref, gb_ref, gc_ref, x_ref, g1_ref,
                  pa_ref, pb_ref, pc_ref, wo_ref, o_ref):
    m = (_sigmoid(ga_ref[...].astype(F32)) * _dot(ya_ref[...], pa_ref[...])
         + _sigmoid(gb_ref[...].astype(F32)) * _dot(yb_ref[...], pb_ref[...])
         + _sigmoid(gc_ref[...].astype(F32)) * _dot(yc_ref[...], pc_ref[...]))
    o_ref[...] = x_ref[...] + g1_ref[...] * _dot(m, wo_ref[...])


def _merge(ya, yb, yc, proj, x, g1, pa, pb, pc, wo):
    B, S, D = x.shape
    tm = min(S, 512)
    W = ya.shape[-1]
    row = lambda b, i: (b, i, 0)
    const = lambda b, i: (0, 0)
    gate = lambda n: pl.BlockSpec((None, tm, D), lambda b, i: (b, i, COL_GATES // D + n))
    return pl.pallas_call(
        _merge_kernel,
        out_shape=jax.ShapeDtypeStruct((B, S, D), F32),
        grid=(B, S // tm),
        in_specs=[pl.BlockSpec((None, tm, W), row), pl.BlockSpec((None, tm, W), row),
                  pl.BlockSpec((None, tm, W), row), gate(0), gate(1), gate(2),
                  pl.BlockSpec((None, tm, D), row),
                  pl.BlockSpec((None, 1, D), lambda b, i: (b, 0, 0)),
                  pl.BlockSpec((W, D), const), pl.BlockSpec((W, D), const), pl.BlockSpec((W, D), const),
                  pl.BlockSpec((D, D), const)],
        out_specs=pl.BlockSpec((None, tm, D), row),
        compiler_params=_cparams("parallel", "parallel"),
        name="branch_merge",
    )(ya, yb, yc, proj, proj, proj, x, g1, pa, pb, pc, wo)


def _ffn_kernel(x_ref, sc_ref, sh_ref, g2_ref, wg_ref, wu_ref, wd_ref, o_ref, h_ref, acc_ref):
    f = pl.program_id(2)

    @pl.when(f == 0)
    def _():
        h = _rms(x_ref[...]) * (1.0 + sc_ref[...]) + sh_ref[...]
        h_ref[...] = h.astype(BF16)
        acc_ref[...] = jnp.zeros_like(acc_ref)

    h = h_ref[...]
    a = _silu(jnp.dot(h, wg_ref[...], preferred_element_type=F32)) \
        * jnp.dot(h, wu_ref[...], preferred_element_type=F32)
    acc_ref[...] += jnp.dot(a.astype(BF16), wd_ref[...], preferred_element_type=F32)

    @pl.when(f == pl.num_programs(2) - 1)
    def _():
        o_ref[...] = x_ref[...] + g2_ref[...] * acc_ref[...]


def _dense_ffn(x, sc, sh, g2, wg, wu, wd):
    B, S, D = x.shape
    F = wg.shape[1]
    tm = min(S, 1024)
    tf = 256
    row = lambda b, i, f: (b, i, 0)
    vec = lambda b, i, f: (b, 0, 0)
    return pl.pallas_call(
        _ffn_kernel,
        out_shape=jax.ShapeDtypeStruct((B, S, D), F32),
        grid=(B, S // tm, F // tf),
        in_specs=[pl.BlockSpec((None, tm, D), row),
                  pl.BlockSpec((None, 1, D), vec), pl.BlockSpec((None, 1, D), vec),
                  pl.BlockSpec((None, 1, D), vec),
                  pl.BlockSpec((D, tf), lambda b, i, f: (0, f)),
                  pl.BlockSpec((D, tf), lambda b, i, f: (0, f)),
                  pl.BlockSpec((tf, D), lambda b, i, f: (f, 0))],
        out_specs=pl.BlockSpec((None, tm, D), row),
        scratch_shapes=[pltpu.VMEM((tm, D), BF16), pltpu.VMEM((tm, D), F32)],
        compiler_params=_cparams("parallel", "parallel", "arbitrary"),
        name="dense_ffn",
    )(x, sc, sh, g2, wg, wu, wd)


def _moe_pre_kernel(x_ref, sc_ref, sh_ref, r_ref, h_ref, top_ref):
    h = _rms(x_ref[...]) * (1.0 + sc_ref[...]) + sh_ref[...]
    h_ref[...] = h
    logits = _dot_hi(h, r_ref[...])
    lane = lax.broadcasted_iota(jnp.int32, logits.shape, 1)
    lane_f = lane.astype(F32)
    logits = jnp.where(lane < N_EXPERTS, logits, -jnp.inf)
    m1 = jnp.max(logits, axis=-1, keepdims=True)
    i1 = jnp.min(jnp.where(logits == m1, lane_f, float(LANES)), axis=-1, keepdims=True)
    rest = jnp.where(lane_f == i1, -jnp.inf, logits)
    m2 = jnp.max(rest, axis=-1, keepdims=True)
    i2 = jnp.min(jnp.where(rest == m2, lane_f, float(LANES)), axis=-1, keepdims=True)
    e2 = jnp.exp(m2 - m1)
    w1 = 1.0 / (1.0 + e2)
    w2 = e2 / (1.0 + e2)
    top_ref[...] = jnp.where(lane == 0, i1, jnp.where(lane == 1, i2,
                             jnp.where(lane == 2, w1, jnp.where(lane == 3, w2, 0.0))))


def _moe_pre(x, sc, sh, router_pad):
    B, S, D = x.shape
    tm = min(S, 512)
    row = lambda b, i: (b, i, 0)
    vec = lambda b, i: (b, 0, 0)
    return pl.pallas_call(
        _moe_pre_kernel,
        out_shape=(jax.ShapeDtypeStruct((B, S, D), F32), jax.ShapeDtypeStruct((B, S, LANES), F32)),
        grid=(B, S // tm),
        in_specs=[pl.BlockSpec((None, tm, D), row), pl.BlockSpec((None, 1, D), vec),
                  pl.BlockSpec((None, 1, D), vec), pl.BlockSpec((D, LANES), lambda b, i: (0, 0))],
        out_specs=(pl.BlockSpec((None, tm, D), row), pl.BlockSpec((None, tm, LANES), row)),
        compiler_params=_cparams("parallel", "parallel"),
        name="moe_route",
    )(x, sc, sh, router_pad)


GATHER_ROWS = 256


def _moe_gather_kernel(cur_ref, nxt_ref, h_ref, o_ref, buf_ref, sem):
    i, n = pl.program_id(0), pl.num_programs(0)

    def issue(idx_ref, slot):
        def body(r, carry):
            pltpu.make_async_copy(h_ref.at[pl.ds(idx_ref[0, 0, r], 1)], buf_ref.at[slot, pl.ds(r, 1)],
                                  sem.at[slot]).start()
            return carry
        lax.fori_loop(0, GATHER_ROWS, body, 0, unroll=8)

    @pl.when(i == 0)
    def _():
        issue(cur_ref, 0)

    @pl.when(i + 1 < n)
    def _():
        issue(nxt_ref, (i + 1) % 2)

    slot = i % 2
    pltpu.make_async_copy(h_ref.at[pl.ds(0, GATHER_ROWS)], buf_ref.at[slot], sem.at[slot]).wait()
    o_ref[...] = buf_ref[slot].astype(BF16)


def _moe_gather(h2d, slot_tok):
    T, D = h2d.shape
    n_blocks = slot_tok.shape[0] // GATHER_ROWS
    idx = slot_tok.reshape(n_blocks, 1, GATHER_ROWS)
    return pl.pallas_call(
        _moe_gather_kernel,
        out_shape=jax.ShapeDtypeStruct((n_blocks * GATHER_ROWS, D), BF16),
        grid=(n_blocks,),
        in_specs=[pl.BlockSpec((1, 1, GATHER_ROWS), lambda i: (i, 0, 0), memory_space=pltpu.SMEM),
                  pl.BlockSpec((1, 1, GATHER_ROWS), lambda i: (jnp.minimum(i + 1, n_blocks - 1), 0, 0),
                               memory_space=pltpu.SMEM),
                  pl.BlockSpec(memory_space=pl.ANY)],
        out_specs=pl.BlockSpec((GATHER_ROWS, D), lambda i: (i, 0)),
        scratch_shapes=[pltpu.VMEM((2, GATHER_ROWS, D), F32), pltpu.SemaphoreType.DMA((2,))],
        compiler_params=_cparams("arbitrary"),
        name="moe_gather",
    )(idx, idx, h2d)


def _moe_up_kernel(be_ref, x_ref, wg_ref, wu_ref, o_ref, wg_bf, wu_bf):
    b = pl.program_id(1)
    prev = be_ref[jnp.maximum(b - 1, 0)]

    @pl.when((b == 0) | (be_ref[b] != prev))
    def _():
        wg_bf[...] = wg_ref[...].astype(BF16)
        wu_bf[...] = wu_ref[...].astype(BF16)

    x = x_ref[...]
    a = _silu(jnp.dot(x, wg_bf[...], preferred_element_type=F32)) \
        * jnp.dot(x, wu_bf[...], preferred_element_type=F32)
    o_ref[...] = a.astype(BF16)


def _moe_up(x_slots, block_e, w_gate, w_up, layer):
    slots, D = x_slots.shape
    F = w_gate.shape[-1]
    tf = 896
    n_blocks = slots // MOE_BLOCK
    wspec = pl.BlockSpec((None, None, D, tf), lambda f, b, be: (layer, be[b], 0, f))
    return pl.pallas_call(
        _moe_up_kernel,
        out_shape=jax.ShapeDtypeStruct((slots, F), BF16),
        grid_spec=pltpu.PrefetchScalarGridSpec(
            num_scalar_prefetch=1,
            grid=(F // tf, n_blocks),
            in_specs=[pl.BlockSpec((MOE_BLOCK, D), lambda f, b, be: (b, 0)), wspec, wspec],
            out_specs=pl.BlockSpec((MOE_BLOCK, tf), lambda f, b, be: (b, f)),
            scratch_shapes=[pltpu.VMEM((D, tf), BF16), pltpu.VMEM((D, tf), BF16)]),
        compiler_params=_cparams("arbitrary", "arbitrary"),
        name="moe_gate_up",
    )(block_e, x_slots, w_gate, w_up)


def _moe_down_kernel(be_ref, a_ref, wd_ref, o_ref, wd_bf):
    b = pl.program_id(1)
    prev = be_ref[jnp.maximum(b - 1, 0)]

    @pl.when((b == 0) | (be_ref[b] != prev))
    def _():
        wd_bf[...] = wd_ref[...].astype(BF16)

    o_ref[...] = jnp.dot(a_ref[...], wd_bf[...], preferred_element_type=F32)


def _moe_down(a_slots, block_e, w_down, layer):
    slots, F = a_slots.shape
    D = w_down.shape[-1]
    tn = 512
    n_blocks = slots // MOE_BLOCK
    return pl.pallas_call(
        _moe_down_kernel,
        out_shape=jax.ShapeDtypeStruct((slots, D), F32),
        grid_spec=pltpu.PrefetchScalarGridSpec(
            num_scalar_prefetch=1,
            grid=(D // tn, n_blocks),
            in_specs=[pl.BlockSpec((MOE_BLOCK, F), lambda n, b, be: (b, 0)),
                      pl.BlockSpec((None, None, F, tn), lambda n, b, be: (layer, be[b], 0, n))],
            out_specs=pl.BlockSpec((MOE_BLOCK, tn), lambda n, b, be: (b, n)),
            scratch_shapes=[pltpu.VMEM((F, tn), BF16)]),
        compiler_params=_cparams("arbitrary", "arbitrary"),
        name="moe_down",
    )(block_e, a_slots, w_down)


def _moe_combine_kernel(cur_ref, nxt_ref, x_ref, g2_ref, top_ref, y_ref, o_ref, buf_ref, sem, *, tm):
    i = pl.program_id(0) * pl.num_programs(1) + pl.program_id(1)
    n = pl.num_programs(0) * pl.num_programs(1)

    def issue(idx_ref, slot):
        def body(r, carry):
            for k in range(TOP_K):
                pltpu.make_async_copy(y_ref.at[pl.ds(idx_ref[0, 0, TOP_K * r + k], 1)],
                                      buf_ref.at[slot, k, pl.ds(r, 1)], sem.at[slot]).start()
            return carry
        lax.fori_loop(0, tm, body, 0, unroll=4)

    @pl.when(i == 0)
    def _():
        issue(cur_ref, 0)

    @pl.when(i + 1 < n)
    def _():
        issue(nxt_ref, (i + 1) % 2)

    slot = i % 2
    for k in range(TOP_K):
        pltpu.make_async_copy(y_ref.at[pl.ds(0, tm)], buf_ref.at[slot, k], sem.at[slot]).wait()
    top = top_ref[...]
    f = top[:, 2:3] * buf_ref[slot, 0] + top[:, 3:4] * buf_ref[slot, 1]
    o_ref[...] = x_ref[...] + g2_ref[...] * f


def _moe_combine(x, g2, top, dest, y_slots):
    B, S, D = x.shape
    tm = min(S, 256)
    nb = S // tm
    row = lambda b, i: (b, i, 0)
    idx = dest.reshape(B * nb, 1, TOP_K * tm)
    return pl.pallas_call(
        functools.partial(_moe_combine_kernel, tm=tm),
        out_shape=jax.ShapeDtypeStruct((B, S, D), F32),
        grid=(B, nb),
        in_specs=[pl.BlockSpec((1, 1, TOP_K * tm), lambda b, i: (b * nb + i, 0, 0), memory_space=pltpu.SMEM),
                  pl.BlockSpec((1, 1, TOP_K * tm), lambda b, i: (jnp.minimum(b * nb + i + 1, B * nb - 1), 0, 0),
                               memory_space=pltpu.SMEM),
                  pl.BlockSpec((None, tm, D), row),
                  pl.BlockSpec((None, 1, D), lambda b, i: (b, 0, 0)),
                  pl.BlockSpec((None, tm, LANES), row),
                  pl.BlockSpec(memory_space=pl.ANY)],
        out_specs=pl.BlockSpec((None, tm, D), row),
        scratch_shapes=[pltpu.VMEM((2, TOP_K, tm, D), F32), pltpu.SemaphoreType.DMA((2,))],
        compiler_params=_cparams("arbitrary", "arbitrary"),
        name="moe_combine",
    )(idx, idx, x, g2, top, y_slots)


def _moe_ffn(x, sc, sh, g2, router, w_gate, w_up, w_down, layer):
    B, S, D = x.shape
    T = B * S
    router_pad = jnp.zeros((D, LANES), F32).at[:, :N_EXPERTS].set(router)
    h, top = _moe_pre(x, sc, sh, router_pad)
    top2d = top.reshape(T, LANES)
    flat_e = top2d[:, :TOP_K].astype(jnp.int32).reshape(-1)
    n_assign = T * TOP_K
    n_blocks = (n_assign + N_EXPERTS * (MOE_BLOCK - 1)) // MOE_BLOCK
    slots = n_blocks * MOE_BLOCK
    onehot = (flat_e[:, None] == jnp.arange(N_EXPERTS, dtype=jnp.int32)[None, :]).astype(jnp.int32)
    csum = jnp.cumsum(onehot, axis=0)
    rank = jnp.sum(csum * onehot, axis=1) - 1
    counts = csum[-1]
    padded = (counts + MOE_BLOCK - 1) // MOE_BLOCK * MOE_BLOCK
    pad_end = jnp.cumsum(padded)
    start_padded = pad_end - padded
    dest = (start_padded[flat_e] + rank).astype(jnp.int32)
    block_start = jnp.arange(n_blocks, dtype=jnp.int32) * MOE_BLOCK
    block_e = jnp.minimum(jnp.sum((block_start[:, None] >= pad_end[None, :]).astype(jnp.int32), axis=1),
                          N_EXPERTS - 1).astype(jnp.int32)
    flat_tok = jnp.repeat(jnp.arange(T, dtype=jnp.int32), TOP_K)
    slot_tok = jnp.zeros((slots,), jnp.int32).at[dest].set(flat_tok)
    x_slots = _moe_gather(h.reshape(T, D), slot_tok)
    a_slots = _moe_up(x_slots, block_e, w_gate, w_up, layer)
    y_slots = _moe_down(a_slots, block_e, w_down, layer)
    return _moe_combine(x, g2, top, dest, y_slots)


def _final_norm_kernel(x_ref, w_ref, o_ref):
    o_ref[...] = _rms(x_ref[...]) * w_ref[...]


def _final_norm(x2d, w):
    T, D = x2d.shape
    tm = min(T, 1024)
    return pl.pallas_call(
        _final_norm_kernel,
        out_shape=jax.ShapeDtypeStruct((T, D), F32),
        grid=(T // tm,),
        in_specs=[pl.BlockSpec((tm, D), lambda i: (i, 0)), pl.BlockSpec((1, D), lambda i: (0, 0))],
        out_specs=pl.BlockSpec((tm, D), lambda i: (i, 0)),
        compiler_params=_cparams("parallel"),
        name="final_norm",
    )(x2d, w.reshape(1, D))


def _arrange_w_in(w):
    D = w.shape[0]
    o = np.cumsum([0, 3 * GDN_DIM, GDN_DIM, GDN_HEADS, GDN_HEADS, SSM_INNER, SSM_CONV_DIM, SSM_HEADS,
                   MLA_Q_LORA, MLA_KV_LORA, MLA_ROPE, 3 * w.shape[0]])
    seg = lambda i: w[:, int(o[i]):int(o[i + 1])]
    qkv, gz, ga, gb, sz, xbc, dt, cq, ckv, kr, gates = [seg(i) for i in range(11)]
    half = MLA_ROPE // 2
    small = jnp.concatenate([ga, gb, dt, jnp.zeros((D, LANES - 2 * GDN_HEADS - SSM_HEADS), w.dtype)], axis=1)
    kr_full = jnp.concatenate([kr, -kr[:, half:], kr[:, :half]], axis=1)
    out = jnp.concatenate([qkv, gz, sz, cq, xbc, gates, ckv, small, kr_full], axis=1)
    assert out.shape[1] == IN_COLS
    return out.astype(BF16)


def _arrange_w_uq(w):
    half = MLA_ROPE // 2
    parts = []
    for h in range(MLA_HEADS):
        c0 = h * (MLA_NOPE + MLA_ROPE)
        rope = w[:, c0 + MLA_NOPE:c0 + MLA_NOPE + MLA_ROPE]
        parts += [w[:, c0:c0 + MLA_NOPE], rope, -rope[:, half:], rope[:, :half]]
    return jnp.concatenate(parts, axis=1).astype(BF16)


def kernel(x, c, positions, w_ada, b_ada, w_in, gdn_conv_w, gdn_a_log, gdn_dt_bias, gdn_norm_w, ssm_conv_w, ssm_conv_b, ssm_a_log, ssm_dt_bias, ssm_d, ssm_norm_w, mla_q_norm_w, mla_w_uq, mla_kv_norm_w, mla_w_uk, mla_w_uv, w_branch_a, w_branch_b, w_branch_c, w_out, ffn_w_gate, ffn_w_up, ffn_w_down, moe_router, moe_w_gate, moe_w_up, moe_w_down, final_norm_w):
    B, S, D = x.shape
    T = B * S
    depth = w_ada.shape[0]
    c_pad = jnp.zeros((SUBLANES, D), F32).at[:B].set(c)
    mod = _modulation(c_pad, w_ada, b_ada)[:, :B].reshape(depth, B, 6, 1, D)
    cos, sin = _rope_tables(positions)
    for l in range(depth):
        sh1, sc1, g1, sh2, sc2, g2 = [mod[l, :, n] for n in range(6)]
        proj, small = _in_proj(x, sc1, sh1, _arrange_w_in(w_in[l]))
        y_a = _gdn(proj, small, gdn_conv_w[l], gdn_a_log[l], gdn_dt_bias[l], gdn_norm_w[l])
        y_b = _ssd(proj, small, ssm_conv_w[l], ssm_conv_b[l], ssm_a_log[l], ssm_dt_bias[l], ssm_d[l], ssm_norm_w[l])
        q, k, v = _mla_proj(proj.reshape(T, IN_COLS), cos, sin, mla_q_norm_w[l], _arrange_w_uq(mla_w_uq[l]),
                            mla_kv_norm_w[l], mla_w_uk[l].astype(BF16), mla_w_uv[l].astype(BF16))
        y_c = _attention(q.reshape(B, S, -1), k.reshape(B, S, -1), v.reshape(B, S, -1))
        x = _merge(y_a, y_b, y_c, proj, x, g1, w_branch_a[l].astype(BF16), w_branch_b[l].astype(BF16),
                   w_branch_c[l].astype(BF16), w_out[l].astype(BF16))
        i = l // 2
        if l % 2 == 0:
            x = _dense_ffn(x, sc2, sh2, g2, ffn_w_gate[i].astype(BF16), ffn_w_up[i].astype(BF16),
                           ffn_w_down[i].astype(BF16))
        else:
            x = _moe_ffn(x, sc2, sh2, g2, moe_router[i], moe_w_gate, moe_w_up, moe_w_down, i)
    return _final_norm(x.reshape(T, D), final_norm_w).reshape(B, S, D)
```

```python
import functools
import math

import numpy as np
import jax
import jax.numpy as jnp
from jax import lax
from jax.experimental import pallas as pl
from jax.experimental.pallas import tpu as pltpu

F32 = jnp.float32
BF16 = jnp.bfloat16
HI = lax.Precision.HIGHEST

EPS = 1e-6
CHUNK = 64
CONV_K = 4
GDN_HEADS = 4
GDN_HEAD_DIM = 128
GDN_DIM = GDN_HEADS * GDN_HEAD_DIM
SSM_HEADS = 8
SSM_HEAD_DIM = 64
SSM_INNER = SSM_HEADS * SSM_HEAD_DIM
SSM_GROUPS = 2
SSM_STATE = 128
SSM_CONV_DIM = SSM_INNER + 2 * SSM_GROUPS * SSM_STATE
MLA_HEADS = 4
MLA_Q_LORA = 512
MLA_KV_LORA = 256
MLA_NOPE = 128
MLA_ROPE = 64
MLA_V = 128
MLA_DIM = MLA_HEADS * MLA_V
ROPE_THETA = 10000.0
N_EXPERTS = 8
TOP_K = 2
MOE_BLOCK = 1024

LANES = 128
SUBLANES = 8
VMEM_LIMIT = 56 * 1024 * 1024

COL_QKV = 0
COL_GZ = 1536
COL_SZ = 2048
COL_CQ = 2560
COL_XBC = 3072
COL_GATES = 4096
COL_CKV = 7168
COL_SMALL = 7424
COL_KR = 7552
IN_COLS = 7680


def _cparams(*sem):
    return pltpu.CompilerParams(dimension_semantics=tuple(sem), vmem_limit_bytes=VMEM_LIMIT)


def _silu(x):
    return x * (1.0 / (1.0 + jnp.exp(-x)))


def _sigmoid(x):
    return 1.0 / (1.0 + jnp.exp(-x))


def _softplus(x):
    return jnp.maximum(x, 0.0) + jnp.log(1.0 + jnp.exp(-jnp.abs(x)))


def _rms(x):
    return x * lax.rsqrt(jnp.mean(x * x, axis=-1, keepdims=True) + EPS)


def _dot(a, b):
    return jnp.dot(a.astype(BF16), b.astype(BF16), preferred_element_type=F32)


def _dot_nt(a, b):
    return lax.dot_general(a.astype(BF16), b.astype(BF16), (((1,), (1,)), ((), ())),
                           preferred_element_type=F32)


def _dot_tn(a, b):
    return lax.dot_general(a.astype(BF16), b.astype(BF16), (((0,), (0,)), ((), ())),
                           preferred_element_type=F32)


def _dot_hi(a, b):
    a_hi, b_hi = a.astype(BF16), b.astype(BF16)
    a_lo = (a - a_hi.astype(F32)).astype(BF16)
    b_lo = (b - b_hi.astype(F32)).astype(BF16)
    d = lambda x, y: jnp.dot(x, y, preferred_element_type=F32)
    return d(a_hi, b_hi) + (d(a_hi, b_lo) + d(a_lo, b_hi))


def _mod_kernel(c_ref, w_ref, b_ref, o_ref):
    cond = _silu(c_ref[...])
    o_ref[...] = _dot(cond, w_ref[...]) + b_ref[...]


def _modulation(c_pad, w_ada, b_ada):
    L, D, N = w_ada.shape
    tn = 1536
    return pl.pallas_call(
        _mod_kernel,
        out_shape=jax.ShapeDtypeStruct((L, SUBLANES, N), F32),
        grid=(L, N // tn),
        in_specs=[pl.BlockSpec((SUBLANES, D), lambda l, j: (0, 0)),
                  pl.BlockSpec((None, D, tn), lambda l, j: (l, 0, j)),
                  pl.BlockSpec((None, 1, tn), lambda l, j: (l, 0, j))],
        out_specs=pl.BlockSpec((None, SUBLANES, tn), lambda l, j: (l, 0, j)),
        compiler_params=_cparams("parallel", "parallel"),
        name="adaln_mod",
    )(c_pad, w_ada, b_ada.reshape(L, 1, N))


def _rope_table_kernel(pos_ref, freq_ref, cos_ref, sin_ref):
    ang = pos_ref[...].astype(F32) * freq_ref[...]
    keep = lax.broadcasted_iota(jnp.int32, ang.shape, 1) < MLA_ROPE
    cos_ref[...] = jnp.where(keep, jnp.cos(ang), 0.0)
    sin_ref[...] = jnp.where(keep, jnp.sin(ang), 0.0)


def _rope_tables(positions):
    T = positions.size
    tm = min(T, 1024)
    inv_freq = ROPE_THETA ** (-jnp.arange(0, MLA_ROPE, 2, dtype=F32) / MLA_ROPE)
    freq = jnp.concatenate([inv_freq, inv_freq, jnp.zeros((LANES - MLA_ROPE,), F32)]).reshape(1, LANES)
    return pl.pallas_call(
        _rope_table_kernel,
        out_shape=(jax.ShapeDtypeStruct((T, LANES), F32), jax.ShapeDtypeStruct((T, LANES), F32)),
        grid=(T // tm,),
        in_specs=[pl.BlockSpec((tm, 1), lambda i: (i, 0)),
                  pl.BlockSpec((1, LANES), lambda i: (0, 0))],
        out_specs=(pl.BlockSpec((tm, LANES), lambda i: (i, 0)),
                   pl.BlockSpec((tm, LANES), lambda i: (i, 0))),
        compiler_params=_cparams("parallel"),
        name="rope_tables",
    )(positions.reshape(T, 1), freq)


IN_TN = 1536
SMALL_TILE = COL_SMALL // IN_TN
SMALL_OFF = COL_SMALL % IN_TN


def _in_proj_kernel(x_ref, sc_ref, sh_ref, w_ref, o_ref, small_ref, h_ref):
    j = pl.program_id(2)

    @pl.when(j == 0)
    def _():
        h = _rms(x_ref[...]) * (1.0 + sc_ref[...]) + sh_ref[...]
        h_ref[...] = h.astype(BF16)

    o = jnp.dot(h_ref[...], w_ref[...], preferred_element_type=F32)
    o_ref[...] = o.astype(BF16)

    @pl.when(j == SMALL_TILE)
    def _():
        small_ref[...] = o[:, SMALL_OFF:SMALL_OFF + LANES]


def _in_proj(x, sc, sh, w):
    B, S, D = x.shape
    N = w.shape[1]
    tm = min(S, 1024)
    tn = IN_TN
    return pl.pallas_call(
        _in_proj_kernel,
        out_shape=(jax.ShapeDtypeStruct((B, S, N), BF16), jax.ShapeDtypeStruct((B, S, LANES), F32)),
        grid=(B, S // tm, N // tn),
        in_specs=[pl.BlockSpec((None, tm, D), lambda b, i, j: (b, i, 0)),
                  pl.BlockSpec((None, 1, D), lambda b, i, j: (b, 0, 0)),
                  pl.BlockSpec((None, 1, D), lambda b, i, j: (b, 0, 0)),
                  pl.BlockSpec((D, tn), lambda b, i, j: (0, j))],
        out_specs=(pl.BlockSpec((None, tm, tn), lambda b, i, j: (b, i, j)),
                   pl.BlockSpec((None, tm, LANES), lambda b, i, j: (b, i, 0))),
        scratch_shapes=[pltpu.VMEM((tm, D), BF16)],
        compiler_params=_cparams("parallel", "parallel", "arbitrary"),
        name="in_proj",
    )(x, sc, sh, w)


def _causal_conv(x, tail_ref, w):
    rows = x.shape[0]
    xe = jnp.concatenate([tail_ref[...], x], axis=0)
    tail_ref[...] = x[rows - SUBLANES:, :]
    y = xe[SUBLANES:, :] * w[CONV_K - 1:CONV_K, :]
    for j in range(CONV_K - 1):
        shift = CONV_K - 1 - j
        y = y + pltpu.roll(xe, shift, axis=0)[SUBLANES:, :] * w[j:j + 1, :]
    return y


def _chunk_cumsum(v, chunk=CHUNK):
    pos = lax.broadcasted_iota(jnp.int32, v.shape, 0) % chunk
    step = 1
    while step < chunk:
        v = v + jnp.where(pos >= step, pltpu.roll(v, step, axis=0), 0.0)
        step *= 2
    return v


def _col_to_row(col_b, eye):
    return jnp.sum(jnp.where(eye, col_b, 0.0), axis=0, keepdims=True)


GDN_CHUNK = 128


def _split(a):
    return a.astype(BF16)


def _mm_split(a, b):
    return jnp.dot(a, b, preferred_element_type=F32)


def _gdn_kernel(qkv_ref, z_ref, sm_ref, cw_ref, par_ref, nw_ref, o_ref,
                tail_ref, state_ref, *, rows):
    C, H, Dh = GDN_CHUNK, GDN_HEADS, GDN_HEAD_DIM
    n_ck = rows // C

    @pl.when(pl.program_id(1) == 0)
    def _():
        tail_ref[...] = jnp.zeros_like(tail_ref)
        state_ref[...] = jnp.zeros_like(state_ref)

    y = _silu(_causal_conv(qkv_ref[...].astype(F32), tail_ref, cw_ref[...]))
    sm = sm_ref[...]
    neg_a = par_ref[0:1, :]
    dt_b = par_ref[1:2, :]
    g_all = _chunk_cumsum(neg_a * _softplus(sm + dt_b), C)
    beta_all = _sigmoid(sm)
    z = z_ref[...].astype(F32)
    nw = nw_ref[...]

    ri = lax.broadcasted_iota(jnp.int32, (C, C), 0)
    ci = lax.broadcasted_iota(jnp.int32, (C, C), 1)
    eye = ri == ci
    incl = ri >= ci
    strict = ri > ci
    eye_f = jnp.where(eye, 1.0, 0.0)

    qn, kn, vn = [], [], []
    for h in range(H):
        q_h = y[:, h * Dh:(h + 1) * Dh]
        k_h = y[:, GDN_DIM + h * Dh:GDN_DIM + (h + 1) * Dh]
        qn.append(q_h * lax.rsqrt(jnp.sum(q_h * q_h, axis=-1, keepdims=True) + EPS) * (Dh ** -0.5))
        kn.append(k_h * lax.rsqrt(jnp.sum(k_h * k_h, axis=-1, keepdims=True) + EPS))
        vn.append(y[:, 2 * GDN_DIM + h * Dh:2 * GDN_DIM + (h + 1) * Dh])

    items = [(c, h) for c in range(n_ck) for h in range(H)]
    p, rhs, qk, q_dec, k_dec, g_end = {}, {}, {}, {}, {}, {}
    for c, h in items:
        r0 = c * C
        q_c, k_c, v_c = qn[h][r0:r0 + C], kn[h][r0:r0 + C], vn[h][r0:r0 + C]
        g_col = g_all[r0:r0 + C, h:h + 1]
        beta = beta_all[r0:r0 + C, H + h:H + h + 1]
        g_b = jnp.broadcast_to(g_col, (C, C))
        g_last = g_col[C - 1:C, :]
        decay = jnp.exp(jnp.where(incl, g_b - _col_to_row(g_b, eye), -1e30))
        eg = jnp.exp(g_col)
        p[c, h] = jnp.where(strict, -(beta * _dot_nt(k_c, k_c) * decay), 0.0)
        qk[c, h] = (_dot_nt(q_c, k_c) * decay).astype(BF16)
        rhs[c, h] = _split(jnp.concatenate([v_c * beta, k_c * (beta * eg)], axis=-1))
        q_dec[c, h] = (q_c * eg).astype(BF16)
        k_dec[c, h] = (k_c * jnp.exp(g_last - g_col)).astype(BF16)
        g_end[c, h] = jnp.exp(g_last)

    inv = {it: eye_f + jnp.where((ri // 2 == ci // 2) & strict, p[it], 0.0) for it in items}
    s = 2
    while s < C:
        level = ((ri // s) % 2 == 1) & ((ci // s) % 2 == 0) & (ri // (2 * s) == ci // (2 * s))
        t_split = {it: _split(inv[it]) for it in items}
        x = {it: _mm_split(_split(jnp.where(level, p[it], 0.0)), t_split[it]) for it in items}
        inv = {it: inv[it] + _mm_split(t_split[it], _split(x[it])) for it in items}
        s *= 2
    sol = {it: _mm_split(_split(inv[it]), rhs[it]) for it in items}

    states = [state_ref[h] for h in range(H)]
    for c in range(n_ck):
        r0 = c * C
        s_bf = [s.astype(BF16) for s in states]
        v_new = [sol[c, h][:, :Dh] - _dot(sol[c, h][:, Dh:], s_bf[h]) for h in range(H)]
        v_bf = [v.astype(BF16) for v in v_new]
        outs = [jnp.dot(q_dec[c, h], s_bf[h], preferred_element_type=F32)
                + jnp.dot(qk[c, h], v_bf[h], preferred_element_type=F32) for h in range(H)]
        states = [states[h] * g_end[c, h]
                  + lax.dot_general(k_dec[c, h], v_bf[h], (((0,), (0,)), ((), ())), preferred_element_type=F32)
                  for h in range(H)]
        for h in range(H):
            z_c = z[r0:r0 + C, h * Dh:(h + 1) * Dh]
            o_ref[r0:r0 + C, h * Dh:(h + 1) * Dh] = (_rms(outs[h]) * nw * _silu(z_c)).astype(BF16)
    for h in range(H):
        state_ref[h] = states[h]


def _gdn(proj, small, conv_w, a_log, dt_bias, norm_w):
    B, S, _ = proj.shape
    rows = min(S, 512)
    par = jnp.zeros((SUBLANES, LANES), F32)
    par = par.at[0, :GDN_HEADS].set(-jnp.exp(a_log)).at[1, :GDN_HEADS].set(dt_bias)
    w3 = 3 * GDN_DIM
    return pl.pallas_call(
        functools.partial(_gdn_kernel, rows=rows),
        out_shape=jax.ShapeDtypeStruct((B, S, GDN_DIM), BF16),
        grid=(B, S // rows),
        in_specs=[pl.BlockSpec((None, rows, w3), lambda b, i: (b, i, COL_QKV // w3)),
                  pl.BlockSpec((None, rows, GDN_DIM), lambda b, i: (b, i, COL_GZ // GDN_DIM)),
                  pl.BlockSpec((None, rows, LANES), lambda b, i: (b, i, 0)),
                  pl.BlockSpec((CONV_K, w3), lambda b, i: (0, 0)),
                  pl.BlockSpec((SUBLANES, LANES), lambda b, i: (0, 0)),
                  pl.BlockSpec((1, GDN_HEAD_DIM), lambda b, i: (0, 0))],
        out_specs=pl.BlockSpec((None, rows, GDN_DIM), lambda b, i: (b, i, 0)),
        scratch_shapes=[pltpu.VMEM((SUBLANES, w3), F32),
                        pltpu.VMEM((GDN_HEADS, GDN_HEAD_DIM, GDN_HEAD_DIM), F32)],
        compiler_params=_cparams("parallel", "arbitrary"),
        name="gated_delta_net",
    )(proj, proj, small, conv_w, par, norm_w.reshape(1, GDN_HEAD_DIM))


SSD_CHUNK = 128


def _ssd_kernel(xbc_ref, z_ref, sm_ref, cw_ref, cb_ref, par_ref, nw_ref, o_ref,
                tail_ref, state_ref, *, rows):
    C, H, P, G, N = SSD_CHUNK, SSM_HEADS, SSM_HEAD_DIM, SSM_GROUPS, SSM_STATE
    E = H // G
    GW = E * P

    @pl.when(pl.program_id(1) == 0)
    def _():
        tail_ref[...] = jnp.zeros_like(tail_ref)
        state_ref[...] = jnp.zeros_like(state_ref)

    y = _silu(_causal_conv(xbc_ref[...].astype(F32), tail_ref, cw_ref[...]) + cb_ref[...])
    sm = sm_ref[...]
    neg_a = par_ref[0:1, :]
    dt_b = par_ref[1:2, :]
    d_skip = par_ref[2:3, :]
    dt_all = _softplus(sm + dt_b)
    acs_all = _chunk_cumsum(dt_all * neg_a, C)
    z = z_ref[...].astype(F32)
    nw = nw_ref[...]

    ri = lax.broadcasted_iota(jnp.int32, (C, C), 0)
    ci = lax.broadcasted_iota(jnp.int32, (C, C), 1)
    eye = ri == ci
    incl = ri >= ci

    items = [(c, g) for c in range(rows // C) for g in range(G)]
    cm, y_local, st_add, e_in, e_out = {}, {}, {}, {}, {}
    for c, g in items:
        r0 = c * C
        bm_c = y[r0:r0 + C, SSM_INNER + g * N:SSM_INNER + (g + 1) * N]
        cm_c = y[r0:r0 + C, SSM_INNER + G * N + g * N:SSM_INNER + G * N + (g + 1) * N]
        x_c = y[r0:r0 + C, g * GW:(g + 1) * GW]
        cb = _dot_nt(cm_c, bm_c)
        xs_parts, ea_parts, ydiag_parts, dx_parts, last_parts = [], [], [], [], []
        for e in range(E):
            lane = SUBLANES + g * E + e
            a_col = acs_all[r0:r0 + C, lane:lane + 1]
            dt_col = dt_all[r0:r0 + C, lane:lane + 1]
            a_b = jnp.broadcast_to(a_col, (C, C))
            a_last = a_col[C - 1:C, :]
            L = jnp.exp(jnp.where(incl, a_b - _col_to_row(a_b, eye), -1e30))
            x_e = x_c[:, e * P:(e + 1) * P]
            xd = x_e * dt_col
            ydiag_parts.append(_dot(cb * L, xd))
            xs_parts.append(xd * jnp.exp(a_last - a_col))
            ea_parts.append(jnp.broadcast_to(jnp.exp(a_col), (C, P)))
            last_parts.append(jnp.broadcast_to(jnp.exp(a_last), (1, P)))
            dx_parts.append(x_e * d_skip[:, lane:lane + 1])
        cm[c, g] = cm_c.astype(BF16)
        y_local[c, g] = jnp.concatenate(ydiag_parts, axis=-1) + jnp.concatenate(dx_parts, axis=-1)
        st_add[c, g] = _dot_tn(bm_c, jnp.concatenate(xs_parts, axis=-1))
        e_in[c, g] = jnp.concatenate(ea_parts, axis=-1)
        e_out[c, g] = jnp.concatenate(last_parts, axis=-1)

    states = [state_ref[g] for g in range(G)]
    for c in range(rows // C):
        r0 = c * C
        for g in range(G):
            y_off = jnp.dot(cm[c, g], states[g].astype(BF16), preferred_element_type=F32)
            y_c = (y_local[c, g] + y_off * e_in[c, g]) * _silu(z[r0:r0 + C, g * GW:(g + 1) * GW])
            o_ref[r0:r0 + C, g * GW:(g + 1) * GW] = (_rms(y_c) * nw[:, g * GW:(g + 1) * GW]).astype(BF16)
            states[g] = states[g] * e_out[c, g] + st_add[c, g]
    for g in range(G):
        state_ref[g] = states[g]


def _ssd(proj, small, conv_w, conv_b, a_log, dt_bias, d_skip, norm_w):
    B, S, _ = proj.shape
    rows = min(S, 512)
    par = jnp.zeros((SUBLANES, LANES), F32)
    sl = slice(SUBLANES, SUBLANES + SSM_HEADS)
    par = par.at[0, sl].set(-jnp.exp(a_log)).at[1, sl].set(dt_bias).at[2, sl].set(d_skip)
    return pl.pallas_call(
        functools.partial(_ssd_kernel, rows=rows),
        out_shape=jax.ShapeDtypeStruct((B, S, SSM_INNER), BF16),
        grid=(B, S // rows),
        in_specs=[pl.BlockSpec((None, rows, SSM_CONV_DIM), lambda b, i: (b, i, COL_XBC // SSM_CONV_DIM)),
                  pl.BlockSpec((None, rows, SSM_INNER), lambda b, i: (b, i, COL_SZ // SSM_INNER)),
                  pl.BlockSpec((None, rows, LANES), lambda b, i: (b, i, 0)),
                  pl.BlockSpec((CONV_K, SSM_CONV_DIM), lambda b, i: (0, 0)),
                  pl.BlockSpec((1, SSM_CONV_DIM), lambda b, i: (0, 0)),
                  pl.BlockSpec((SUBLANES, LANES), lambda b, i: (0, 0)),
                  pl.BlockSpec((1, SSM_INNER), lambda b, i: (0, 0))],
        out_specs=pl.BlockSpec((None, rows, SSM_INNER), lambda b, i: (b, i, 0)),
        scratch_shapes=[pltpu.VMEM((SUBLANES, SSM_CONV_DIM), F32),
                        pltpu.VMEM((SSM_GROUPS, SSM_STATE, SSM_INNER // SSM_GROUPS), F32)],
        compiler_params=_cparams("parallel", "arbitrary"),
        name="mamba2_ssd",
    )(proj, proj, small, conv_w, conv_b.reshape(1, -1), par, norm_w.reshape(1, -1))


QK_W = 2 * LANES


def _rope(x, cos, sin):
    return x * cos + pltpu.roll(x, MLA_ROPE, axis=1) * sin


def _mla_proj_kernel(cq_ref, ckv_ref, kr_ref, cos_ref, sin_ref, qw_ref, wq_ref, kvw_ref, wk_ref, wv_ref,
                     q_ref, k_ref, v_ref):
    scale = (MLA_NOPE + MLA_ROPE) ** -0.5
    cos, sin = cos_ref[...], sin_ref[...]
    qc = _rms(cq_ref[...].astype(F32)) * qw_ref[...]
    qf = _dot(qc, wq_ref[...]) * scale
    ckv = (_rms(ckv_ref[...].astype(F32)) * kvw_ref[...]).astype(BF16)
    kn = jnp.dot(ckv, wk_ref[...], preferred_element_type=F32)
    v_ref[...] = jnp.dot(ckv, wv_ref[...], preferred_element_type=F32).astype(BF16)
    kr = _rope(kr_ref[...].astype(F32), cos, sin).astype(BF16)
    for h in range(MLA_HEADS):
        c0 = h * QK_W
        q_ref[:, c0:c0 + LANES] = qf[:, c0:c0 + LANES].astype(BF16)
        q_ref[:, c0 + LANES:c0 + QK_W] = _rope(qf[:, c0 + LANES:c0 + QK_W], cos, sin).astype(BF16)
        k_ref[:, c0:c0 + LANES] = kn[:, h * MLA_NOPE:(h + 1) * MLA_NOPE].astype(BF16)
        k_ref[:, c0 + LANES:c0 + QK_W] = kr


def _mla_proj(proj2d, cos, sin, q_norm_w, wq, kv_norm_w, wk, wv):
    T = proj2d.shape[0]
    tm = min(T, 512)
    HQ = MLA_HEADS * QK_W
    row = lambda i: (i, 0)
    const = lambda i: (0, 0)
    return pl.pallas_call(
        _mla_proj_kernel,
        out_shape=(jax.ShapeDtypeStruct((T, HQ), BF16), jax.ShapeDtypeStruct((T, HQ), BF16),
                   jax.ShapeDtypeStruct((T, MLA_DIM), BF16)),
        grid=(T // tm,),
        in_specs=[pl.BlockSpec((tm, MLA_Q_LORA), lambda i: (i, COL_CQ // MLA_Q_LORA)),
                  pl.BlockSpec((tm, MLA_KV_LORA), lambda i: (i, COL_CKV // MLA_KV_LORA)),
                  pl.BlockSpec((tm, LANES), lambda i: (i, COL_KR // LANES)),
                  pl.BlockSpec((tm, LANES), row),
                  pl.BlockSpec((tm, LANES), row),
                  pl.BlockSpec((1, MLA_Q_LORA), const),
                  pl.BlockSpec((MLA_Q_LORA, HQ), const),
                  pl.BlockSpec((1, MLA_KV_LORA), const),
                  pl.BlockSpec((MLA_KV_LORA, MLA_HEADS * MLA_NOPE), const),
                  pl.BlockSpec((MLA_KV_LORA, MLA_DIM), const)],
        out_specs=(pl.BlockSpec((tm, HQ), row), pl.BlockSpec((tm, HQ), row),
                   pl.BlockSpec((tm, MLA_DIM), row)),
        compiler_params=_cparams("parallel"),
        name="mla_proj",
    )(proj2d, proj2d, proj2d, cos, sin, q_norm_w.reshape(1, -1), wq, kv_norm_w.reshape(1, -1), wk, wv)


def _attn_kernel(q_ref, k_ref, v_ref, o_ref, m_ref, l_ref, acc_ref, *, t):
    qi = pl.program_id(1)
    m_ref[...] = jnp.full_like(m_ref, -1e30)
    l_ref[...] = jnp.zeros_like(l_ref)
    acc_ref[...] = jnp.zeros_like(acc_ref)

    def block(j, diagonal):
        k0 = pl.multiple_of(j * t, t)
        hs = range(MLA_HEADS)
        s = [lax.dot_general(q_ref[:, h * QK_W:(h + 1) * QK_W], k_ref[pl.ds(k0, t), h * QK_W:(h + 1) * QK_W],
                             (((1,), (1,)), ((), ())), preferred_element_type=F32) for h in hs]
        if diagonal:
            row = lax.broadcasted_iota(jnp.int32, s[0].shape, 0)
            col = lax.broadcasted_iota(jnp.int32, s[0].shape, 1)
            s = [jnp.where(row >= col, s[h], -1e30) for h in hs]
        cols = range(t // LANES)
        sc = [[s[h][:, c * LANES:(c + 1) * LANES] for c in cols] for h in hs]
        m_prev = [m_ref[h] for h in hs]
        m_new = [jnp.maximum(m_prev[h], jnp.max(functools.reduce(jnp.maximum, sc[h]), axis=-1, keepdims=True))
                 for h in hs]
        p = [[jnp.exp(sc[h][c] - m_new[h]) for c in cols] for h in hs]
        alpha = [jnp.exp(m_prev[h] - m_new[h]) for h in hs]
        pv = [jnp.dot(jnp.concatenate(p[h], axis=-1).astype(BF16), v_ref[pl.ds(k0, t), h * MLA_V:(h + 1) * MLA_V],
                      preferred_element_type=F32) for h in hs]
        for h in hs:
            l_ref[h] = alpha[h] * l_ref[h] + functools.reduce(jnp.add, p[h])
            acc_ref[h] = alpha[h] * acc_ref[h] + pv[h]
            m_ref[h] = m_new[h]

    def body(j, carry):
        block(j, False)
        return carry

    lax.fori_loop(0, qi, body, 0)
    block(qi, True)
    for h in range(MLA_HEADS):
        o_ref[:, h * MLA_V:(h + 1) * MLA_V] = (acc_ref[h] / jnp.sum(l_ref[h], axis=-1, keepdims=True)).astype(BF16)


def _attention(q, k, v):
    B, S, _ = q.shape
    t = min(S, 512)
    H = MLA_HEADS
    return pl.pallas_call(
        functools.partial(_attn_kernel, t=t),
        out_shape=jax.ShapeDtypeStruct((B, S, MLA_DIM), BF16),
        grid=(B, S // t),
        in_specs=[pl.BlockSpec((None, t, H * QK_W), lambda b, qi: (b, qi, 0)),
                  pl.BlockSpec((None, S, H * QK_W), lambda b, qi: (b, 0, 0)),
                  pl.BlockSpec((None, S, MLA_DIM), lambda b, qi: (b, 0, 0))],
        out_specs=pl.BlockSpec((None, t, MLA_DIM), lambda b, qi: (b, qi, 0)),
        scratch_shapes=[pltpu.VMEM((H, t, LANES), F32), pltpu.VMEM((H, t, LANES), F32),
                        pltpu.VMEM((H, t, MLA_V), F32)],
        compiler_params=_cparams("parallel", "arbitrary"),
        name="mla_attention",
    )(q, k, v)


def _merge_kernel(ya_ref, yb_ref, yc_ref, ga_ref, gb_ref, gc_ref, x_ref, g1_ref,
                  pa_ref, pb_ref, pc_ref, wo_ref, o_ref):
    m = (_sigmoid(ga_ref[...].astype(F32)) * _dot(ya_ref[...], pa_ref[...])
         + _sigmoid(gb_ref[...].astype(F32)) * _dot(yb_ref[...], pb_ref[...])
         + _sigmoid(gc_ref[...].astype(F32)) * _dot(yc_ref[...], pc_ref[...]))
    o_ref[...] = x_ref[...] + g1_ref[...] * _dot(m, wo_ref[...])


def _merge(ya, yb, yc, proj, x, g1, pa, pb, pc, wo):
    B, S, D = x.shape
    tm = min(S, 512)
    W = ya.shape[-1]
    row = lambda b, i: (b, i, 0)
    const = lambda b, i: (0, 0)
    gate = lambda n: pl.BlockSpec((None, tm, D), lambda b, i: (b, i, COL_GATES // D + n))
    return pl.pallas_call(
        _merge_kernel,
        out_shape=jax.ShapeDtypeStruct((B, S, D), F32),
        grid=(B, S // tm),
        in_specs=[pl.BlockSpec((None, tm, W), row), pl.BlockSpec((None, tm, W), row),
                  pl.BlockSpec((None, tm, W), row), gate(0), gate(1), gate(2),
                  pl.BlockSpec((None, tm, D), row),
                  pl.BlockSpec((None, 1, D), lambda b, i: (b, 0, 0)),
                  pl.BlockSpec((W, D), const), pl.BlockSpec((W, D), const), pl.BlockSpec((W, D), const),
                  pl.BlockSpec((D, D), const)],
        out_specs=pl.BlockSpec((None, tm, D), row),
        compiler_params=_cparams("parallel", "parallel"),
        name="branch_merge",
    )(ya, yb, yc, proj, proj, proj, x, g1, pa, pb, pc, wo)


def _ffn_kernel(x_ref, sc_ref, sh_ref, g2_ref, wg_ref, wu_ref, wd_ref, o_ref, h_ref, acc_ref):
    f = pl.program_id(2)

    @pl.when(f == 0)
    def _():
        h = _rms(x_ref[...]) * (1.0 + sc_ref[...]) + sh_ref[...]
        h_ref[...] = h.astype(BF16)
        acc_ref[...] = jnp.zeros_like(acc_ref)

    h = h_ref[...]
    a = _silu(_dot(h, wg_ref[...])) * _dot(h, wu_ref[...])
    acc_ref[...] += _dot(a, wd_ref[...])

    @pl.when(f == pl.num_programs(2) - 1)
    def _():
        o_ref[...] = x_ref[...] + g2_ref[...] * acc_ref[...]


def _dense_ffn(x, sc, sh, g2, wg, wu, wd, layer):
    B, S, D = x.shape
    F = wg.shape[-1]
    tm = min(S, 1024)
    tf = 256
    row = lambda b, i, f: (b, i, 0)
    vec = lambda b, i, f: (b, 0, 0)
    return pl.pallas_call(
        _ffn_kernel,
        out_shape=jax.ShapeDtypeStruct((B, S, D), F32),
        grid=(B, S // tm, F // tf),
        in_specs=[pl.BlockSpec((None, tm, D), row),
                  pl.BlockSpec((None, 1, D), vec), pl.BlockSpec((None, 1, D), vec),
                  pl.BlockSpec((None, 1, D), vec),
                  pl.BlockSpec((None, D, tf), lambda b, i, f: (layer, 0, f)),
                  pl.BlockSpec((None, D, tf), lambda b, i, f: (layer, 0, f)),
                  pl.BlockSpec((None, tf, D), lambda b, i, f: (layer, f, 0))],
        out_specs=pl.BlockSpec((None, tm, D), row),
        scratch_shapes=[pltpu.VMEM((tm, D), BF16), pltpu.VMEM((tm, D), F32)],
        compiler_params=_cparams("parallel", "parallel", "arbitrary"),
        name="dense_ffn",
    )(x, sc, sh, g2, wg, wu, wd)


def _moe_pre_kernel(x_ref, sc_ref, sh_ref, r_ref, h_ref, top_ref):
    h = _rms(x_ref[...]) * (1.0 + sc_ref[...]) + sh_ref[...]
    h_ref[...] = h
    logits = _dot_hi(h, r_ref[...])
    lane = lax.broadcasted_iota(jnp.int32, logits.shape, 1)
    lane_f = lane.astype(F32)
    logits = jnp.where(lane < N_EXPERTS, logits, -jnp.inf)
    m1 = jnp.max(logits, axis=-1, keepdims=True)
    i1 = jnp.min(jnp.where(logits == m1, lane_f, float(LANES)), axis=-1, keepdims=True)
    rest = jnp.where(lane_f == i1, -jnp.inf, logits)
    m2 = jnp.max(rest, axis=-1, keepdims=True)
    i2 = jnp.min(jnp.where(rest == m2, lane_f, float(LANES)), axis=-1, keepdims=True)
    e2 = jnp.exp(m2 - m1)
    w1 = 1.0 / (1.0 + e2)
    w2 = e2 / (1.0 + e2)
    top_ref[...] = jnp.where(lane == 0, i1, jnp.where(lane == 1, i2,
                             jnp.where(lane == 2, w1, jnp.where(lane == 3, w2, 0.0))))


def _moe_pre(x, sc, sh, router_pad):
    B, S, D = x.shape
    tm = min(S, 512)
    row = lambda b, i: (b, i, 0)
    vec = lambda b, i: (b, 0, 0)
    return pl.pallas_call(
        _moe_pre_kernel,
        out_shape=(jax.ShapeDtypeStruct((B, S, D), F32), jax.ShapeDtypeStruct((B, S, LANES), F32)),
        grid=(B, S // tm),
        in_specs=[pl.BlockSpec((None, tm, D), row), pl.BlockSpec((None, 1, D), vec),
                  pl.BlockSpec((None, 1, D), vec), pl.BlockSpec((D, LANES), lambda b, i: (0, 0))],
        out_specs=(pl.BlockSpec((None, tm, D), row), pl.BlockSpec((None, tm, LANES), row)),
        compiler_params=_cparams("parallel", "parallel"),
        name="moe_route",
    )(x, sc, sh, router_pad)


GATHER_ROWS = 256


def _moe_gather_kernel(cur_ref, nxt_ref, h_ref, o_ref, buf_ref, sem):
    i, n = pl.program_id(0), pl.num_programs(0)

    def issue(idx_ref, slot):
        def body(g, carry):
            for u in range(2):
                r = 2 * g + u
                pltpu.make_async_copy(h_ref.at[pl.ds(idx_ref[0, 0, r], 1)], buf_ref.at[slot, pl.ds(r, 1)],
                                      sem.at[slot]).start(priority=u)
            return carry
        lax.fori_loop(0, GATHER_ROWS // 2, body, 0, unroll=4)

    @pl.when(i == 0)
    def _():
        issue(cur_ref, 0)

    @pl.when(i + 1 < n)
    def _():
        issue(nxt_ref, (i + 1) % 2)

    slot = i % 2
    pltpu.make_async_copy(h_ref.at[pl.ds(0, GATHER_ROWS)], buf_ref.at[slot], sem.at[slot]).wait()
    o_ref[...] = buf_ref[slot].astype(BF16)


def _moe_gather(h2d, slot_tok):
    T, D = h2d.shape
    n_blocks = slot_tok.shape[0] // GATHER_ROWS
    idx = slot_tok.reshape(n_blocks, 1, GATHER_ROWS)
    return pl.pallas_call(
        _moe_gather_kernel,
        out_shape=jax.ShapeDtypeStruct((n_blocks * GATHER_ROWS, D), BF16),
        grid=(n_blocks,),
        in_specs=[pl.BlockSpec((1, 1, GATHER_ROWS), lambda i: (i, 0, 0), memory_space=pltpu.SMEM),
                  pl.BlockSpec((1, 1, GATHER_ROWS), lambda i: (jnp.minimum(i + 1, n_blocks - 1), 0, 0),
                               memory_space=pltpu.SMEM),
                  pl.BlockSpec(memory_space=pl.ANY)],
        out_specs=pl.BlockSpec((GATHER_ROWS, D), lambda i: (i, 0)),
        scratch_shapes=[pltpu.VMEM((2, GATHER_ROWS, D), F32), pltpu.SemaphoreType.DMA((2,))],
        compiler_params=_cparams("arbitrary"),
        name="moe_gather",
    )(idx, idx, h2d)


def _moe_up_kernel(be_ref, x_ref, wg_ref, wu_ref, o_ref, wg_bf, wu_bf):
    b = pl.program_id(1)
    prev = be_ref[jnp.maximum(b - 1, 0)]

    @pl.when((b == 0) | (be_ref[b] != prev))
    def _():
        wg_bf[...] = wg_ref[...].astype(BF16)
        wu_bf[...] = wu_ref[...].astype(BF16)

    x = x_ref[...]
    a = _silu(jnp.dot(x, wg_bf[...], preferred_element_type=F32)) \
        * jnp.dot(x, wu_bf[...], preferred_element_type=F32)
    o_ref[...] = a.astype(BF16)


def _moe_up(x_slots, block_e, w_gate, w_up, layer):
    slots, D = x_slots.shape
    F = w_gate.shape[-1]
    tf = 896
    n_blocks = slots // MOE_BLOCK
    wspec = pl.BlockSpec((None, None, D, tf), lambda f, b, be: (layer, be[b], 0, f))
    return pl.pallas_call(
        _moe_up_kernel,
        out_shape=jax.ShapeDtypeStruct((slots, F), BF16),
        grid_spec=pltpu.PrefetchScalarGridSpec(
            num_scalar_prefetch=1,
            grid=(F // tf, n_blocks),
            in_specs=[pl.BlockSpec((MOE_BLOCK, D), lambda f, b, be: (b, 0)), wspec, wspec],
            out_specs=pl.BlockSpec((MOE_BLOCK, tf), lambda f, b, be: (b, f)),
            scratch_shapes=[pltpu.VMEM((D, tf), BF16), pltpu.VMEM((D, tf), BF16)]),
        compiler_params=_cparams("arbitrary", "arbitrary"),
        name="moe_gate_up",
    )(block_e, x_slots, w_gate, w_up)


def _moe_down_kernel(be_ref, a_ref, wd_ref, o_ref, wd_bf):
    b = pl.program_id(1)
    prev = be_ref[jnp.maximum(b - 1, 0)]

    @pl.when((b == 0) | (be_ref[b] != prev))
    def _():
        wd_bf[...] = wd_ref[...].astype(BF16)

    o_ref[...] = jnp.dot(a_ref[...], wd_bf[...], preferred_element_type=F32)


def _moe_down(a_slots, block_e, w_down, layer):
    slots, F = a_slots.shape
    D = w_down.shape[-1]
    tn = 512
    n_blocks = slots // MOE_BLOCK
    return pl.pallas_call(
        _moe_down_kernel,
        out_shape=jax.ShapeDtypeStruct((slots, D), F32),
        grid_spec=pltpu.PrefetchScalarGridSpec(
            num_scalar_prefetch=1,
            grid=(D // tn, n_blocks),
            in_specs=[pl.BlockSpec((MOE_BLOCK, F), lambda n, b, be: (b, 0)),
                      pl.BlockSpec((None, None, F, tn), lambda n, b, be: (layer, be[b], 0, n))],
            out_specs=pl.BlockSpec((MOE_BLOCK, tn), lambda n, b, be: (b, n)),
            scratch_shapes=[pltpu.VMEM((F, tn), BF16)]),
        compiler_params=_cparams("arbitrary", "arbitrary"),
        name="moe_down",
    )(block_e, a_slots, w_down)


def _moe_combine_kernel(cur_ref, nxt_ref, x_ref, g2_ref, top_ref, y_ref, o_ref, buf_ref, sem, *, tm):
    i = pl.program_id(0) * pl.num_programs(1) + pl.program_id(1)
    n = pl.num_programs(0) * pl.num_programs(1)

    def issue(idx_ref, slot):
        def body(r, carry):
            for k in range(TOP_K):
                pltpu.make_async_copy(y_ref.at[pl.ds(idx_ref[0, 0, TOP_K * r + k], 1)],
                                      buf_ref.at[slot, k, pl.ds(r, 1)], sem.at[slot]).start(priority=k)
            return carry
        lax.fori_loop(0, tm, body, 0, unroll=4)

    @pl.when(i == 0)
    def _():
        issue(cur_ref, 0)

    @pl.when(i + 1 < n)
    def _():
        issue(nxt_ref, (i + 1) % 2)

    slot = i % 2
    for k in range(TOP_K):
        pltpu.make_async_copy(y_ref.at[pl.ds(0, tm)], buf_ref.at[slot, k], sem.at[slot]).wait()
    top = top_ref[...]
    f = top[:, 2:3] * buf_ref[slot, 0] + top[:, 3:4] * buf_ref[slot, 1]
    o_ref[...] = x_ref[...] + g2_ref[...] * f


def _moe_combine(x, g2, top, dest, y_slots):
    B, S, D = x.shape
    tm = min(S, 256)
    nb = S // tm
    row = lambda b, i: (b, i, 0)
    idx = dest.reshape(B * nb, 1, TOP_K * tm)
    return pl.pallas_call(
        functools.partial(_moe_combine_kernel, tm=tm),
        out_shape=jax.ShapeDtypeStruct((B, S, D), F32),
        grid=(B, nb),
        in_specs=[pl.BlockSpec((1, 1, TOP_K * tm), lambda b, i: (b * nb + i, 0, 0), memory_space=pltpu.SMEM),
                  pl.BlockSpec((1, 1, TOP_K * tm), lambda b, i: (jnp.minimum(b * nb + i + 1, B * nb - 1), 0, 0),
                               memory_space=pltpu.SMEM),
                  pl.BlockSpec((None, tm, D), row),
                  pl.BlockSpec((None, 1, D), lambda b, i: (b, 0, 0)),
                  pl.BlockSpec((None, tm, LANES), row),
                  pl.BlockSpec(memory_space=pl.ANY)],
        out_specs=pl.BlockSpec((None, tm, D), row),
        scratch_shapes=[pltpu.VMEM((2, TOP_K, tm, D), F32), pltpu.SemaphoreType.DMA((2,))],
        compiler_params=_cparams("arbitrary", "arbitrary"),
        name="moe_combine",
    )(idx, idx, x, g2, top, y_slots)


def _moe_ffn(x, sc, sh, g2, router, w_gate, w_up, w_down, layer):
    B, S, D = x.shape
    T = B * S
    router_pad = jnp.zeros((D, LANES), F32).at[:, :N_EXPERTS].set(router)
    h, top = _moe_pre(x, sc, sh, router_pad)
    top2d = top.reshape(T, LANES)
    flat_e = top2d[:, :TOP_K].astype(jnp.int32).reshape(-1)
    n_assign = T * TOP_K
    n_blocks = (n_assign + N_EXPERTS * (MOE_BLOCK - 1)) // MOE_BLOCK
    slots = n_blocks * MOE_BLOCK
    onehot = (flat_e[:, None] == jnp.arange(N_EXPERTS, dtype=jnp.int32)[None, :]).astype(jnp.int32)
    csum = jnp.cumsum(onehot, axis=0)
    rank = jnp.sum(csum * onehot, axis=1) - 1
    counts = csum[-1]
    padded = (counts + MOE_BLOCK - 1) // MOE_BLOCK * MOE_BLOCK
    pad_end = jnp.cumsum(padded)
    start_padded = pad_end - padded
    dest = (start_padded[flat_e] + rank).astype(jnp.int32)
    block_start = jnp.arange(n_blocks, dtype=jnp.int32) * MOE_BLOCK
    block_e = jnp.minimum(jnp.sum((block_start[:, None] >= pad_end[None, :]).astype(jnp.int32), axis=1),
                          N_EXPERTS - 1).astype(jnp.int32)
    flat_tok = jnp.repeat(jnp.arange(T, dtype=jnp.int32), TOP_K)
    slot_tok = (jnp.arange(slots, dtype=jnp.int32) % T).at[dest].set(flat_tok)
    x_slots = _moe_gather(h.reshape(T, D), slot_tok)
    a_slots = _moe_up(x_slots, block_e, w_gate, w_up, layer)
    y_slots = _moe_down(a_slots, block_e, w_down, layer)
    return _moe_combine(x, g2, top, dest, y_slots)


def _final_norm_kernel(x_ref, w_ref, o_ref):
    o_ref[...] = _rms(x_ref[...]) * w_ref[...]


def _final_norm(x2d, w):
    T, D = x2d.shape
    tm = min(T, 1024)
    return pl.pallas_call(
        _final_norm_kernel,
        out_shape=jax.ShapeDtypeStruct((T, D), F32),
        grid=(T // tm,),
        in_specs=[pl.BlockSpec((tm, D), lambda i: (i, 0)), pl.BlockSpec((1, D), lambda i: (0, 0))],
        out_specs=pl.BlockSpec((tm, D), lambda i: (i, 0)),
        compiler_params=_cparams("parallel"),
        name="final_norm",
    )(x2d, w.reshape(1, D))


def _arrange_w_in(w):
    D = w.shape[0]
    o = np.cumsum([0, 3 * GDN_DIM, GDN_DIM, GDN_HEADS, GDN_HEADS, SSM_INNER, SSM_CONV_DIM, SSM_HEADS,
                   MLA_Q_LORA, MLA_KV_LORA, MLA_ROPE, 3 * w.shape[0]])
    seg = lambda i: w[:, int(o[i]):int(o[i + 1])]
    qkv, gz, ga, gb, sz, xbc, dt, cq, ckv, kr, gates = [seg(i) for i in range(11)]
    half = MLA_ROPE // 2
    small = jnp.concatenate([ga, gb, dt, jnp.zeros((D, LANES - 2 * GDN_HEADS - SSM_HEADS), w.dtype)], axis=1)
    kr_full = jnp.concatenate([kr, -kr[:, half:], kr[:, :half]], axis=1)
    out = jnp.concatenate([qkv, gz, sz, cq, xbc, gates, ckv, small, kr_full], axis=1)
    assert out.shape[1] == IN_COLS
    return out.astype(BF16)


def _arrange_w_uq(w):
    half = MLA_ROPE // 2
    parts = []
    for h in range(MLA_HEADS):
        c0 = h * (MLA_NOPE + MLA_ROPE)
        rope = w[:, c0 + MLA_NOPE:c0 + MLA_NOPE + MLA_ROPE]
        parts += [w[:, c0:c0 + MLA_NOPE], rope, -rope[:, half:], rope[:, :half]]
    return jnp.concatenate(parts, axis=1).astype(BF16)


def kernel(x, c, positions, w_ada, b_ada, w_in, gdn_conv_w, gdn_a_log, gdn_dt_bias, gdn_norm_w, ssm_conv_w, ssm_conv_b, ssm_a_log, ssm_dt_bias, ssm_d, ssm_norm_w, mla_q_norm_w, mla_w_uq, mla_kv_norm_w, mla_w_uk, mla_w_uv, w_branch_a, w_branch_b, w_branch_c, w_out, ffn_w_gate, ffn_w_up, ffn_w_down, moe_router, moe_w_gate, moe_w_up, moe_w_down, final_norm_w):
    B, S, D = x.shape
    T = B * S
    depth = w_ada.shape[0]
    c_pad = jnp.zeros((SUBLANES, D), F32).at[:B].set(c)
    mod = _modulation(c_pad, w_ada, b_ada)[:, :B].reshape(depth, B, 6, 1, D)
    cos, sin = _rope_tables(positions)
    for l in range(depth):
        sh1, sc1, g1, sh2, sc2, g2 = [mod[l, :, n] for n in range(6)]
        proj, small = _in_proj(x, sc1, sh1, _arrange_w_in(w_in[l]))
        y_a = _gdn(proj, small, gdn_conv_w[l], gdn_a_log[l], gdn_dt_bias[l], gdn_norm_w[l])
        y_b = _ssd(proj, small, ssm_conv_w[l], ssm_conv_b[l], ssm_a_log[l], ssm_dt_bias[l], ssm_d[l], ssm_norm_w[l])
        q, k, v = _mla_proj(proj.reshape(T, IN_COLS), cos, sin, mla_q_norm_w[l], _arrange_w_uq(mla_w_uq[l]),
                            mla_kv_norm_w[l], mla_w_uk[l].astype(BF16), mla_w_uv[l].astype(BF16))
        y_c = _attention(q.reshape(B, S, -1), k.reshape(B, S, -1), v.reshape(B, S, -1))
        x = _merge(y_a, y_b, y_c, proj, x, g1, w_branch_a[l].astype(BF16), w_branch_b[l].astype(BF16),
                   w_branch_c[l].astype(BF16), w_out[l].astype(BF16))
        i = l // 2
        if l % 2 == 0:
            x = _dense_ffn(x, sc2, sh2, g2, ffn_w_gate, ffn_w_up, ffn_w_down, i)
        else:
            x = _moe_ffn(x, sc2, sh2, g2, moe_router[i], moe_w_gate, moe_w_up, moe_w_down, i)
    return _final_norm(x.reshape(T, D), final_norm_w).reshape(B, S, D)
```

```python
import functools
import math

import numpy as np
import jax
import jax.numpy as jnp
from jax import lax
from jax.experimental import pallas as pl
from jax.experimental.pallas import tpu as pltpu

F32 = jnp.float32
BF16 = jnp.bfloat16
HI = lax.Precision.HIGHEST

EPS = 1e-6
CHUNK = 64
CONV_K = 4
GDN_HEADS = 4
GDN_HEAD_DIM = 128
GDN_DIM = GDN_HEADS * GDN_HEAD_DIM
SSM_HEADS = 8
SSM_HEAD_DIM = 64
SSM_INNER = SSM_HEADS * SSM_HEAD_DIM
SSM_GROUPS = 2
SSM_STATE = 128
SSM_CONV_DIM = SSM_INNER + 2 * SSM_GROUPS * SSM_STATE
MLA_HEADS = 4
MLA_Q_LORA = 512
MLA_KV_LORA = 256
MLA_NOPE = 128
MLA_ROPE = 64
MLA_V = 128
MLA_DIM = MLA_HEADS * MLA_V
ROPE_THETA = 10000.0
N_EXPERTS = 8
TOP_K = 2
MOE_BLOCK = 1024

LANES = 128
SUBLANES = 8
VMEM_LIMIT = 56 * 1024 * 1024

COL_QKV = 0
COL_GZ = 1536
COL_SZ = 2048
COL_CQ = 2560
COL_XBC = 3072
COL_GATES = 4096
COL_CKV = 7168
COL_SMALL = 7424
COL_KR = 7552
IN_COLS = 7680


def _cparams(*sem):
    return pltpu.CompilerParams(dimension_semantics=tuple(sem), vmem_limit_bytes=VMEM_LIMIT)


def _silu(x):
    return x * (1.0 / (1.0 + jnp.exp(-x)))


def _sigmoid(x):
    return 1.0 / (1.0 + jnp.exp(-x))


def _softplus(x):
    return jnp.maximum(x, 0.0) + jnp.log(1.0 + jnp.exp(-jnp.abs(x)))


def _rms(x):
    return x * lax.rsqrt(jnp.mean(x * x, axis=-1, keepdims=True) + EPS)


def _dot(a, b):
    return jnp.dot(a.astype(BF16), b.astype(BF16), preferred_element_type=F32)


def _dot_nt(a, b):
    return lax.dot_general(a.astype(BF16), b.astype(BF16), (((1,), (1,)), ((), ())),
                           preferred_element_type=F32)


def _dot_tn(a, b):
    return lax.dot_general(a.astype(BF16), b.astype(BF16), (((0,), (0,)), ((), ())),
                           preferred_element_type=F32)


def _dot_hi(a, b):
    a_hi, b_hi = a.astype(BF16), b.astype(BF16)
    a_lo = (a - a_hi.astype(F32)).astype(BF16)
    b_lo = (b - b_hi.astype(F32)).astype(BF16)
    d = lambda x, y: jnp.dot(x, y, preferred_element_type=F32)
    return d(a_hi, b_hi) + (d(a_hi, b_lo) + d(a_lo, b_hi))


def _mod_kernel(c_ref, w_ref, b_ref, o_ref):
    cond = _silu(c_ref[...])
    o_ref[...] = _dot(cond, w_ref[...]) + b_ref[...]


def _modulation(c_pad, w_ada, b_ada):
    L, D, N = w_ada.shape
    tn = 1536
    return pl.pallas_call(
        _mod_kernel,
        out_shape=jax.ShapeDtypeStruct((L, SUBLANES, N), F32),
        grid=(L, N // tn),
        in_specs=[pl.BlockSpec((SUBLANES, D), lambda l, j: (0, 0)),
                  pl.BlockSpec((None, D, tn), lambda l, j: (l, 0, j)),
                  pl.BlockSpec((None, 1, tn), lambda l, j: (l, 0, j))],
        out_specs=pl.BlockSpec((None, SUBLANES, tn), lambda l, j: (l, 0, j)),
        compiler_params=_cparams("parallel", "parallel"),
        name="adaln_mod",
    )(c_pad, w_ada, b_ada.reshape(L, 1, N))


def _rope_table_kernel(pos_ref, freq_ref, cos_ref, sin_ref):
    ang = pos_ref[...].astype(F32) * freq_ref[...]
    keep = lax.broadcasted_iota(jnp.int32, ang.shape, 1) < MLA_ROPE
    cos_ref[...] = jnp.where(keep, jnp.cos(ang), 0.0)
    sin_ref[...] = jnp.where(keep, jnp.sin(ang), 0.0)


def _rope_tables(positions):
    T = positions.size
    tm = min(T, 1024)
    inv_freq = ROPE_THETA ** (-jnp.arange(0, MLA_ROPE, 2, dtype=F32) / MLA_ROPE)
    freq = jnp.concatenate([inv_freq, inv_freq, jnp.zeros((LANES - MLA_ROPE,), F32)]).reshape(1, LANES)
    return pl.pallas_call(
        _rope_table_kernel,
        out_shape=(jax.ShapeDtypeStruct((T, LANES), F32), jax.ShapeDtypeStruct((T, LANES), F32)),
        grid=(T // tm,),
        in_specs=[pl.BlockSpec((tm, 1), lambda i: (i, 0)),
                  pl.BlockSpec((1, LANES), lambda i: (0, 0))],
        out_specs=(pl.BlockSpec((tm, LANES), lambda i: (i, 0)),
                   pl.BlockSpec((tm, LANES), lambda i: (i, 0))),
        compiler_params=_cparams("parallel"),
        name="rope_tables",
    )(positions.reshape(T, 1), freq)


IN_TN = 2560
SMALL_TILE = COL_SMALL // IN_TN
SMALL_OFF = COL_SMALL % IN_TN


def _in_proj_kernel(x_ref, sc_ref, sh_ref, w_ref, o_ref, small_ref, h_ref):
    j = pl.program_id(2)

    @pl.when(j == 0)
    def _():
        h = _rms(x_ref[...]) * (1.0 + sc_ref[...]) + sh_ref[...]
        h_ref[...] = h.astype(BF16)

    o = jnp.dot(h_ref[...], w_ref[...], preferred_element_type=F32)
    o_ref[...] = o.astype(BF16)

    @pl.when(j == SMALL_TILE)
    def _():
        small_ref[...] = o[:, SMALL_OFF:SMALL_OFF + LANES]


def _in_proj(x, sc, sh, w):
    B, S, D = x.shape
    N = w.shape[1]
    tm = min(S, 1024)
    tn = IN_TN
    return pl.pallas_call(
        _in_proj_kernel,
        out_shape=(jax.ShapeDtypeStruct((B, S, N), BF16), jax.ShapeDtypeStruct((B, S, LANES), F32)),
        grid=(B, S // tm, N // tn),
        in_specs=[pl.BlockSpec((None, tm, D), lambda b, i, j: (b, i, 0)),
                  pl.BlockSpec((None, 1, D), lambda b, i, j: (b, 0, 0)),
                  pl.BlockSpec((None, 1, D), lambda b, i, j: (b, 0, 0)),
                  pl.BlockSpec((D, tn), lambda b, i, j: (0, j))],
        out_specs=(pl.BlockSpec((None, tm, tn), lambda b, i, j: (b, i, j)),
                   pl.BlockSpec((None, tm, LANES), lambda b, i, j: (b, i, 0))),
        scratch_shapes=[pltpu.VMEM((tm, D), BF16)],
        compiler_params=_cparams("parallel", "parallel", "arbitrary"),
        name="in_proj",
    )(x, sc, sh, w)


def _causal_conv(x, tail_ref, w):
    rows = x.shape[0]
    xe = jnp.concatenate([tail_ref[...], x], axis=0)
    tail_ref[...] = x[rows - SUBLANES:, :]
    y = xe[SUBLANES:, :] * w[CONV_K - 1:CONV_K, :]
    for j in range(CONV_K - 1):
        shift = CONV_K - 1 - j
        y = y + pltpu.roll(xe, shift, axis=0)[SUBLANES:, :] * w[j:j + 1, :]
    return y


def _chunk_cumsum(v, chunk=CHUNK):
    pos = lax.broadcasted_iota(jnp.int32, v.shape, 0) % chunk
    step = 1
    while step < chunk:
        v = v + jnp.where(pos >= step, pltpu.roll(v, step, axis=0), 0.0)
        step *= 2
    return v


def _col_to_row(col_b, eye):
    return jnp.sum(jnp.where(eye, col_b, 0.0), axis=0, keepdims=True)


GDN_CHUNK = 128


def _split(a):
    return a.astype(BF16)


def _mm_split(a, b):
    return jnp.dot(a, b, preferred_element_type=F32)


def _gdn_kernel(qkv_ref, z_ref, sm_ref, cw_ref, par_ref, nw_ref, o_ref,
                tail_ref, state_ref, *, rows):
    C, H, Dh = GDN_CHUNK, GDN_HEADS, GDN_HEAD_DIM
    n_ck = rows // C

    @pl.when(pl.program_id(1) == 0)
    def _():
        tail_ref[...] = jnp.zeros_like(tail_ref)
        state_ref[...] = jnp.zeros_like(state_ref)

    y = _silu(_causal_conv(qkv_ref[...].astype(F32), tail_ref, cw_ref[...]))
    sm = sm_ref[...]
    neg_a = par_ref[0:1, :]
    dt_b = par_ref[1:2, :]
    g_all = _chunk_cumsum(neg_a * _softplus(sm + dt_b), C)
    beta_all = _sigmoid(sm)
    z = z_ref[...].astype(F32)
    nw = nw_ref[...]

    ri = lax.broadcasted_iota(jnp.int32, (C, C), 0)
    ci = lax.broadcasted_iota(jnp.int32, (C, C), 1)
    eye = ri == ci
    incl = ri >= ci
    strict = ri > ci
    eye_f = jnp.where(eye, 1.0, 0.0)

    qn, kn, vn = [], [], []
    for h in range(H):
        q_h = y[:, h * Dh:(h + 1) * Dh]
        k_h = y[:, GDN_DIM + h * Dh:GDN_DIM + (h + 1) * Dh]
        qn.append(q_h * lax.rsqrt(jnp.sum(q_h * q_h, axis=-1, keepdims=True) + EPS) * (Dh ** -0.5))
        kn.append(k_h * lax.rsqrt(jnp.sum(k_h * k_h, axis=-1, keepdims=True) + EPS))
        vn.append(y[:, 2 * GDN_DIM + h * Dh:2 * GDN_DIM + (h + 1) * Dh])

    items = [(c, h) for c in range(n_ck) for h in range(H)]
    p, rhs, qk, q_dec, k_dec, g_end = {}, {}, {}, {}, {}, {}
    for c, h in items:
        r0 = c * C
        q_c, k_c, v_c = qn[h][r0:r0 + C], kn[h][r0:r0 + C], vn[h][r0:r0 + C]
        g_col = g_all[r0:r0 + C, h:h + 1]
        beta = beta_all[r0:r0 + C, H + h:H + h + 1]
        g_b = jnp.broadcast_to(g_col, (C, C))
        g_last = g_col[C - 1:C, :]
        decay = jnp.exp(jnp.where(incl, g_b - _col_to_row(g_b, eye), -1e30))
        eg = jnp.exp(g_col)
        p[c, h] = jnp.where(strict, -(beta * _dot_nt(k_c, k_c) * decay), 0.0)
        qk[c, h] = (_dot_nt(q_c, k_c) * decay).astype(BF16)
        rhs[c, h] = _split(jnp.concatenate([v_c * beta, k_c * (beta * eg)], axis=-1))
        q_dec[c, h] = (q_c * eg).astype(BF16)
        k_dec[c, h] = (k_c * jnp.exp(g_last - g_col)).astype(BF16)
        g_end[c, h] = jnp.exp(g_last)

    inv = {it: eye_f + jnp.where((ri // 2 == ci // 2) & strict, p[it], 0.0) for it in items}
    s = 2
    while s < C:
        level = ((ri // s) % 2 == 1) & ((ci // s) % 2 == 0) & (ri // (2 * s) == ci // (2 * s))
        t_split = {it: _split(inv[it]) for it in items}
        x = {it: _mm_split(_split(jnp.where(level, p[it], 0.0)), t_split[it]) for it in items}
        inv = {it: inv[it] + _mm_split(t_split[it], _split(x[it])) for it in items}
        s *= 2
    sol = {it: _mm_split(_split(inv[it]), rhs[it]) for it in items}

    states = [state_ref[h] for h in range(H)]
    for c in range(n_ck):
        r0 = c * C
        s_bf = [s.astype(BF16) for s in states]
        v_new = [sol[c, h][:, :Dh] - _dot(sol[c, h][:, Dh:], s_bf[h]) for h in range(H)]
        v_bf = [v.astype(BF16) for v in v_new]
        outs = [jnp.dot(q_dec[c, h], s_bf[h], preferred_element_type=F32)
                + jnp.dot(qk[c, h], v_bf[h], preferred_element_type=F32) for h in range(H)]
        states = [states[h] * g_end[c, h]
                  + lax.dot_general(k_dec[c, h], v_bf[h], (((0,), (0,)), ((), ())), preferred_element_type=F32)
                  for h in range(H)]
        for h in range(H):
            z_c = z[r0:r0 + C, h * Dh:(h + 1) * Dh]
            o_ref[r0:r0 + C, h * Dh:(h + 1) * Dh] = (_rms(outs[h]) * nw * _silu(z_c)).astype(BF16)
    for h in range(H):
        state_ref[h] = states[h]


def _gdn(proj, small, conv_w, a_log, dt_bias, norm_w):
    B, S, _ = proj.shape
    rows = min(S, 512)
    par = jnp.zeros((SUBLANES, LANES), F32)
    par = par.at[0, :GDN_HEADS].set(-jnp.exp(a_log)).at[1, :GDN_HEADS].set(dt_bias)
    w3 = 3 * GDN_DIM
    return pl.pallas_call(
        functools.partial(_gdn_kernel, rows=rows),
        out_shape=jax.ShapeDtypeStruct((B, S, GDN_DIM), BF16),
        grid=(B, S // rows),
        in_specs=[pl.BlockSpec((None, rows, w3), lambda b, i: (b, i, COL_QKV // w3)),
                  pl.BlockSpec((None, rows, GDN_DIM), lambda b, i: (b, i, COL_GZ // GDN_DIM)),
                  pl.BlockSpec((None, rows, LANES), lambda b, i: (b, i, 0)),
                  pl.BlockSpec((CONV_K, w3), lambda b, i: (0, 0)),
                  pl.BlockSpec((SUBLANES, LANES), lambda b, i: (0, 0)),
                  pl.BlockSpec((1, GDN_HEAD_DIM), lambda b, i: (0, 0))],
        out_specs=pl.BlockSpec((None, rows, GDN_DIM), lambda b, i: (b, i, 0)),
        scratch_shapes=[pltpu.VMEM((SUBLANES, w3), F32),
                        pltpu.VMEM((GDN_HEADS, GDN_HEAD_DIM, GDN_HEAD_DIM), F32)],
        compiler_params=_cparams("parallel", "arbitrary"),
        name="gated_delta_net",
    )(proj, proj, small, conv_w, par, norm_w.reshape(1, GDN_HEAD_DIM))


SSD_CHUNK = 128


def _ssd_kernel(xbc_ref, z_ref, sm_ref, cw_ref, cb_ref, par_ref, nw_ref, o_ref,
                tail_ref, state_ref, *, rows):
    C, H, P, G, N = SSD_CHUNK, SSM_HEADS, SSM_HEAD_DIM, SSM_GROUPS, SSM_STATE
    E = H // G
    GW = E * P

    @pl.when(pl.program_id(1) == 0)
    def _():
        tail_ref[...] = jnp.zeros_like(tail_ref)
        state_ref[...] = jnp.zeros_like(state_ref)

    y = _silu(_causal_conv(xbc_ref[...].astype(F32), tail_ref, cw_ref[...]) + cb_ref[...])
    sm = sm_ref[...]
    neg_a = par_ref[0:1, :]
    dt_b = par_ref[1:2, :]
    d_skip = par_ref[2:3, :]
    dt_all = _softplus(sm + dt_b)
    acs_all = _chunk_cumsum(dt_all * neg_a, C)
    z = z_ref[...].astype(F32)
    nw = nw_ref[...]

    ri = lax.broadcasted_iota(jnp.int32, (C, C), 0)
    ci = lax.broadcasted_iota(jnp.int32, (C, C), 1)
    eye = ri == ci
    incl = ri >= ci

    items = [(c, g) for c in range(rows // C) for g in range(G)]
    cm, y_local, st_add, e_in, e_out = {}, {}, {}, {}, {}
    for c, g in items:
        r0 = c * C
        bm_c = y[r0:r0 + C, SSM_INNER + g * N:SSM_INNER + (g + 1) * N]
        cm_c = y[r0:r0 + C, SSM_INNER + G * N + g * N:SSM_INNER + G * N + (g + 1) * N]
        x_c = y[r0:r0 + C, g * GW:(g + 1) * GW]
        cb = _dot_nt(cm_c, bm_c)
        xs_parts, ea_parts, ydiag_parts, dx_parts, last_parts = [], [], [], [], []
        for e in range(E):
            lane = SUBLANES + g * E + e
            a_col = acs_all[r0:r0 + C, lane:lane + 1]
            dt_col = dt_all[r0:r0 + C, lane:lane + 1]
            a_b = jnp.broadcast_to(a_col, (C, C))
            a_last = a_col[C - 1:C, :]
            L = jnp.exp(jnp.where(incl, a_b - _col_to_row(a_b, eye), -1e30))
            x_e = x_c[:, e * P:(e + 1) * P]
            xd = x_e * dt_col
            ydiag_parts.append(_dot(cb * L, xd))
            xs_parts.append(xd * jnp.exp(a_last - a_col))
            ea_parts.append(jnp.broadcast_to(jnp.exp(a_col), (C, P)))
            last_parts.append(jnp.broadcast_to(jnp.exp(a_last), (1, P)))
            dx_parts.append(x_e * d_skip[:, lane:lane + 1])
        cm[c, g] = cm_c.astype(BF16)
        y_local[c, g] = jnp.concatenate(ydiag_parts, axis=-1) + jnp.concatenate(dx_parts, axis=-1)
        st_add[c, g] = _dot_tn(bm_c, jnp.concatenate(xs_parts, axis=-1))
        e_in[c, g] = jnp.concatenate(ea_parts, axis=-1)
        e_out[c, g] = jnp.concatenate(last_parts, axis=-1)

    states = [state_ref[g] for g in range(G)]
    for c in range(rows // C):
        r0 = c * C
        for g in range(G):
            y_off = jnp.dot(cm[c, g], states[g].astype(BF16), preferred_element_type=F32)
            y_c = (y_local[c, g] + y_off * e_in[c, g]) * _silu(z[r0:r0 + C, g * GW:(g + 1) * GW])
            o_ref[r0:r0 + C, g * GW:(g + 1) * GW] = (_rms(y_c) * nw[:, g * GW:(g + 1) * GW]).astype(BF16)
            states[g] = states[g] * e_out[c, g] + st_add[c, g]
    for g in range(G):
        state_ref[g] = states[g]


def _ssd(proj, small, conv_w, conv_b, a_log, dt_bias, d_skip, norm_w):
    B, S, _ = proj.shape
    rows = min(S, 512)
    par = jnp.zeros((SUBLANES, LANES), F32)
    sl = slice(SUBLANES, SUBLANES + SSM_HEADS)
    par = par.at[0, sl].set(-jnp.exp(a_log)).at[1, sl].set(dt_bias).at[2, sl].set(d_skip)
    return pl.pallas_call(
        functools.partial(_ssd_kernel, rows=rows),
        out_shape=jax.ShapeDtypeStruct((B, S, SSM_INNER), BF16),
        grid=(B, S // rows),
        in_specs=[pl.BlockSpec((None, rows, SSM_CONV_DIM), lambda b, i: (b, i, COL_XBC // SSM_CONV_DIM)),
                  pl.BlockSpec((None, rows, SSM_INNER), lambda b, i: (b, i, COL_SZ // SSM_INNER)),
                  pl.BlockSpec((None, rows, LANES), lambda b, i: (b, i, 0)),
                  pl.BlockSpec((CONV_K, SSM_CONV_DIM), lambda b, i: (0, 0)),
                  pl.BlockSpec((1, SSM_CONV_DIM), lambda b, i: (0, 0)),
                  pl.BlockSpec((SUBLANES, LANES), lambda b, i: (0, 0)),
                  pl.BlockSpec((1, SSM_INNER), lambda b, i: (0, 0))],
        out_specs=pl.BlockSpec((None, rows, SSM_INNER), lambda b, i: (b, i, 0)),
        scratch_shapes=[pltpu.VMEM((SUBLANES, SSM_CONV_DIM), F32),
                        pltpu.VMEM((SSM_GROUPS, SSM_STATE, SSM_INNER // SSM_GROUPS), F32)],
        compiler_params=_cparams("parallel", "arbitrary"),
        name="mamba2_ssd",
    )(proj, proj, small, conv_w, conv_b.reshape(1, -1), par, norm_w.reshape(1, -1))


QK_W = 2 * LANES


def _rope(x, cos, sin):
    return x * cos + pltpu.roll(x, MLA_ROPE, axis=1) * sin


def _mla_proj_kernel(cq_ref, ckv_ref, kr_ref, cos_ref, sin_ref, qw_ref, wq_ref, kvw_ref, wk_ref, wv_ref,
                     q_ref, k_ref, v_ref):
    scale = (MLA_NOPE + MLA_ROPE) ** -0.5 * math.log2(math.e)
    cos, sin = cos_ref[...], sin_ref[...]
    qc = _rms(cq_ref[...].astype(F32)) * qw_ref[...]
    qf = _dot(qc, wq_ref[...]) * scale
    ckv = (_rms(ckv_ref[...].astype(F32)) * kvw_ref[...]).astype(BF16)
    kn = jnp.dot(ckv, wk_ref[...], preferred_element_type=F32)
    v_ref[...] = jnp.dot(ckv, wv_ref[...], preferred_element_type=F32).astype(BF16)
    kr = _rope(kr_ref[...].astype(F32), cos, sin).astype(BF16)
    for h in range(MLA_HEADS):
        c0 = h * QK_W
        q_ref[:, c0:c0 + LANES] = qf[:, c0:c0 + LANES].astype(BF16)
        q_ref[:, c0 + LANES:c0 + QK_W] = _rope(qf[:, c0 + LANES:c0 + QK_W], cos, sin).astype(BF16)
        k_ref[:, c0:c0 + LANES] = kn[:, h * MLA_NOPE:(h + 1) * MLA_NOPE].astype(BF16)
        k_ref[:, c0 + LANES:c0 + QK_W] = kr


def _mla_proj(proj2d, cos, sin, q_norm_w, wq, kv_norm_w, wk, wv):
    T = proj2d.shape[0]
    tm = min(T, 512)
    HQ = MLA_HEADS * QK_W
    row = lambda i: (i, 0)
    const = lambda i: (0, 0)
    return pl.pallas_call(
        _mla_proj_kernel,
        out_shape=(jax.ShapeDtypeStruct((T, HQ), BF16), jax.ShapeDtypeStruct((T, HQ), BF16),
                   jax.ShapeDtypeStruct((T, MLA_DIM), BF16)),
        grid=(T // tm,),
        in_specs=[pl.BlockSpec((tm, MLA_Q_LORA), lambda i: (i, COL_CQ // MLA_Q_LORA)),
                  pl.BlockSpec((tm, MLA_KV_LORA), lambda i: (i, COL_CKV // MLA_KV_LORA)),
                  pl.BlockSpec((tm, LANES), lambda i: (i, COL_KR // LANES)),
                  pl.BlockSpec((tm, LANES), row),
                  pl.BlockSpec((tm, LANES), row),
                  pl.BlockSpec((1, MLA_Q_LORA), const),
                  pl.BlockSpec((MLA_Q_LORA, HQ), const),
                  pl.BlockSpec((1, MLA_KV_LORA), const),
                  pl.BlockSpec((MLA_KV_LORA, MLA_HEADS * MLA_NOPE), const),
                  pl.BlockSpec((MLA_KV_LORA, MLA_DIM), const)],
        out_specs=(pl.BlockSpec((tm, HQ), row), pl.BlockSpec((tm, HQ), row),
                   pl.BlockSpec((tm, MLA_DIM), row)),
        compiler_params=_cparams("parallel"),
        name="mla_proj",
    )(proj2d, proj2d, proj2d, cos, sin, q_norm_w.reshape(1, -1), wq, kv_norm_w.reshape(1, -1), wk, wv)


def _attn_kernel(q_ref, k_ref, v_ref, o_ref, m_ref, l_ref, acc_ref, *, t):
    qi = pl.program_id(1)
    m_ref[...] = jnp.full_like(m_ref, -1e30)
    l_ref[...] = jnp.zeros_like(l_ref)
    acc_ref[...] = jnp.zeros_like(acc_ref)

    def block(j, diagonal):
        k0 = pl.multiple_of(j * t, t)
        hs = range(MLA_HEADS)
        s = [lax.dot_general(q_ref[:, h * QK_W:(h + 1) * QK_W], k_ref[pl.ds(k0, t), h * QK_W:(h + 1) * QK_W],
                             (((1,), (1,)), ((), ())), preferred_element_type=F32) for h in hs]
        if diagonal:
            row = lax.broadcasted_iota(jnp.int32, s[0].shape, 0)
            col = lax.broadcasted_iota(jnp.int32, s[0].shape, 1)
            s = [jnp.where(row >= col, s[h], -1e30) for h in hs]
        cols = range(t // LANES)
        sc = [[s[h][:, c * LANES:(c + 1) * LANES] for c in cols] for h in hs]
        m_prev = [m_ref[h] for h in hs]
        m_new = [jnp.maximum(m_prev[h], jnp.max(functools.reduce(jnp.maximum, sc[h]), axis=-1, keepdims=True))
                 for h in hs]
        p = [[jnp.exp2(sc[h][c] - m_new[h]) for c in cols] for h in hs]
        alpha = [jnp.exp2(m_prev[h] - m_new[h]) for h in hs]
        pv = [jnp.dot(jnp.concatenate(p[h], axis=-1).astype(BF16), v_ref[pl.ds(k0, t), h * MLA_V:(h + 1) * MLA_V],
                      preferred_element_type=F32) for h in hs]
        for h in hs:
            l_ref[h] = alpha[h] * l_ref[h] + functools.reduce(jnp.add, p[h])
            acc_ref[h] = alpha[h] * acc_ref[h] + pv[h]
            m_ref[h] = m_new[h]

    def body(j, carry):
        block(j, False)
        return carry

    lax.fori_loop(0, qi, body, 0)
    block(qi, True)
    for h in range(MLA_HEADS):
        o_ref[:, h * MLA_V:(h + 1) * MLA_V] = (acc_ref[h] / jnp.sum(l_ref[h], axis=-1, keepdims=True)).astype(BF16)


def _attention(q, k, v):
    B, S, _ = q.shape
    t = min(S, 512)
    H = MLA_HEADS
    return pl.pallas_call(
        functools.partial(_attn_kernel, t=t),
        out_shape=jax.ShapeDtypeStruct((B, S, MLA_DIM), BF16),
        grid=(B, S // t),
        in_specs=[pl.BlockSpec((None, t, H * QK_W), lambda b, qi: (b, qi, 0)),
                  pl.BlockSpec((None, S, H * QK_W), lambda b, qi: (b, 0, 0)),
                  pl.BlockSpec((None, S, MLA_DIM), lambda b, qi: (b, 0, 0))],
        out_specs=pl.BlockSpec((None, t, MLA_DIM), lambda b, qi: (b, qi, 0)),
        scratch_shapes=[pltpu.VMEM((H, t, LANES), F32), pltpu.VMEM((H, t, LANES), F32),
                        pltpu.VMEM((H, t, MLA_V), F32)],
        compiler_params=_cparams("parallel", "arbitrary"),
        name="mla_attention",
    )(q, k, v)


def _merge_kernel(ya_ref, yb_ref, yc_ref, ga_ref, gb_ref, gc_ref, x_ref, g1_ref,
                  pa_ref, pb_ref, pc_ref, wo_ref, o_ref):
    m = (_sigmoid(ga_ref[...].astype(F32)) * _dot(ya_ref[...], pa_ref[...])
         + _sigmoid(gb_ref[...].astype(F32)) * _dot(yb_ref[...], pb_ref[...])
         + _sigmoid(gc_ref[...].astype(F32)) * _dot(yc_ref[...], pc_ref[...]))
    o_ref[...] = x_ref[...] + g1_ref[...] * _dot(m, wo_ref[...])


def _merge(ya, yb, yc, proj, x, g1, pa, pb, pc, wo):
    B, S, D = x.shape
    tm = min(S, 512)
    W = ya.shape[-1]
    row = lambda b, i: (b, i, 0)
    const = lambda b, i: (0, 0)
    gate = lambda n: pl.BlockSpec((None, tm, D), lambda b, i: (b, i, COL_GATES // D + n))
    return pl.pallas_call(
        _merge_kernel,
        out_shape=jax.ShapeDtypeStruct((B, S, D), F32),
        grid=(B, S // tm),
        in_specs=[pl.BlockSpec((None, tm, W), row), pl.BlockSpec((None, tm, W), row),
                  pl.BlockSpec((None, tm, W), row), gate(0), gate(1), gate(2),
                  pl.BlockSpec((None, tm, D), row),
                  pl.BlockSpec((None, 1, D), lambda b, i: (b, 0, 0)),
                  pl.BlockSpec((W, D), const), pl.BlockSpec((W, D), const), pl.BlockSpec((W, D), const),
                  pl.BlockSpec((D, D), const)],
        out_specs=pl.BlockSpec((None, tm, D), row),
        compiler_params=_cparams("parallel", "parallel"),
        name="branch_merge",
    )(ya, yb, yc, proj, proj, proj, x, g1, pa, pb, pc, wo)


def _ffn_kernel(x_ref, sc_ref, sh_ref, g2_ref, wg_ref, wu_ref, wd_ref, o_ref, h_ref, acc_ref):
    f = pl.program_id(2)

    @pl.when(f == 0)
    def _():
        h = _rms(x_ref[...]) * (1.0 + sc_ref[...]) + sh_ref[...]
        h_ref[...] = h.astype(BF16)
        acc_ref[...] = jnp.zeros_like(acc_ref)

    h = h_ref[...]
    a = _silu(_dot(h, wg_ref[...])) * _dot(h, wu_ref[...])
    acc_ref[...] += _dot(a, wd_ref[...])

    @pl.when(f == pl.num_programs(2) - 1)
    def _():
        o_ref[...] = x_ref[...] + g2_ref[...] * acc_ref[...]


def _dense_ffn(x, sc, sh, g2, wg, wu, wd, layer):
    B, S, D = x.shape
    F = wg.shape[-1]
    tm = min(S, 1024)
    tf = 256
    row = lambda b, i, f: (b, i, 0)
    vec = lambda b, i, f: (b, 0, 0)
    return pl.pallas_call(
        _ffn_kernel,
        out_shape=jax.ShapeDtypeStruct((B, S, D), F32),
        grid=(B, S // tm, F // tf),
        in_specs=[pl.BlockSpec((None, tm, D), row),
                  pl.BlockSpec((None, 1, D), vec), pl.BlockSpec((None, 1, D), vec),
                  pl.BlockSpec((None, 1, D), vec),
                  pl.BlockSpec((None, D, tf), lambda b, i, f: (layer, 0, f)),
                  pl.BlockSpec((None, D, tf), lambda b, i, f: (layer, 0, f)),
                  pl.BlockSpec((None, tf, D), lambda b, i, f: (layer, f, 0))],
        out_specs=pl.BlockSpec((None, tm, D), row),
        scratch_shapes=[pltpu.VMEM((tm, D), BF16), pltpu.VMEM((tm, D), F32)],
        compiler_params=_cparams("parallel", "parallel", "arbitrary"),
        name="dense_ffn",
    )(x, sc, sh, g2, wg, wu, wd)


def _moe_pre_kernel(x_ref, sc_ref, sh_ref, r_ref, h_ref, top_ref):
    h = _rms(x_ref[...]) * (1.0 + sc_ref[...]) + sh_ref[...]
    h_ref[...] = h
    logits = _dot_hi(h, r_ref[...])
    lane = lax.broadcasted_iota(jnp.int32, logits.shape, 1)
    lane_f = lane.astype(F32)
    logits = jnp.where(lane < N_EXPERTS, logits, -jnp.inf)
    m1 = jnp.max(logits, axis=-1, keepdims=True)
    i1 = jnp.min(jnp.where(logits == m1, lane_f, float(LANES)), axis=-1, keepdims=True)
    rest = jnp.where(lane_f == i1, -jnp.inf, logits)
    m2 = jnp.max(rest, axis=-1, keepdims=True)
    i2 = jnp.min(jnp.where(rest == m2, lane_f, float(LANES)), axis=-1, keepdims=True)
    e2 = jnp.exp(m2 - m1)
    w1 = 1.0 / (1.0 + e2)
    w2 = e2 / (1.0 + e2)
    top_ref[...] = jnp.where(lane == 0, i1, jnp.where(lane == 1, i2,
                             jnp.where(lane == 2, w1, jnp.where(lane == 3, w2, 0.0))))


def _moe_pre(x, sc, sh, router_pad):
    B, S, D = x.shape
    tm = min(S, 512)
    row = lambda b, i: (b, i, 0)
    vec = lambda b, i: (b, 0, 0)
    return pl.pallas_call(
        _moe_pre_kernel,
        out_shape=(jax.ShapeDtypeStruct((B, S, D), F32), jax.ShapeDtypeStruct((B, S, LANES), F32)),
        grid=(B, S // tm),
        in_specs=[pl.BlockSpec((None, tm, D), row), pl.BlockSpec((None, 1, D), vec),
                  pl.BlockSpec((None, 1, D), vec), pl.BlockSpec((D, LANES), lambda b, i: (0, 0))],
        out_specs=(pl.BlockSpec((None, tm, D), row), pl.BlockSpec((None, tm, LANES), row)),
        compiler_params=_cparams("parallel", "parallel"),
        name="moe_route",
    )(x, sc, sh, router_pad)


GATHER_ROWS = 512


def _moe_gather_kernel(cur_ref, nxt_ref, h_ref, o_ref, buf_ref, sem):
    i, n = pl.program_id(0), pl.num_programs(0)

    def issue(idx_ref, slot):
        def body(g, carry):
            for u in range(2):
                r = 2 * g + u
                pltpu.make_async_copy(h_ref.at[pl.ds(idx_ref[0, 0, r], 1)], buf_ref.at[slot, pl.ds(r, 1)],
                                      sem.at[slot]).start(priority=u)
            return carry
        lax.fori_loop(0, GATHER_ROWS // 2, body, 0, unroll=4)

    @pl.when(i == 0)
    def _():
        issue(cur_ref, 0)

    @pl.when(i + 1 < n)
    def _():
        issue(nxt_ref, (i + 1) % 2)

    slot = i % 2
    pltpu.make_async_copy(h_ref.at[pl.ds(0, GATHER_ROWS)], buf_ref.at[slot], sem.at[slot]).wait()
    o_ref[...] = buf_ref[slot].astype(BF16)


def _moe_gather(h2d, slot_tok):
    T, D = h2d.shape
    n_blocks = slot_tok.shape[0] // GATHER_ROWS
    idx = slot_tok.reshape(n_blocks, 1, GATHER_ROWS)
    return pl.pallas_call(
        _moe_gather_kernel,
        out_shape=jax.ShapeDtypeStruct((n_blocks * GATHER_ROWS, D), BF16),
        grid=(n_blocks,),
        in_specs=[pl.BlockSpec((1, 1, GATHER_ROWS), lambda i: (i, 0, 0), memory_space=pltpu.SMEM),
                  pl.BlockSpec((1, 1, GATHER_ROWS), lambda i: (jnp.minimum(i + 1, n_blocks - 1), 0, 0),
                               memory_space=pltpu.SMEM),
                  pl.BlockSpec(memory_space=pl.ANY)],
        out_specs=pl.BlockSpec((GATHER_ROWS, D), lambda i: (i, 0)),
        scratch_shapes=[pltpu.VMEM((2, GATHER_ROWS, D), F32), pltpu.SemaphoreType.DMA((2,))],
        compiler_params=_cparams("arbitrary"),
        name="moe_gather",
    )(idx, idx, h2d)


def _moe_up_kernel(be_ref, x_ref, wg_ref, wu_ref, o_ref, wg_bf, wu_bf):
    b = pl.program_id(1)
    prev = be_ref[jnp.maximum(b - 1, 0)]

    @pl.when((b == 0) | (be_ref[b] != prev))
    def _():
        wg_bf[...] = wg_ref[...].astype(BF16)
        wu_bf[...] = wu_ref[...].astype(BF16)

    x = x_ref[...]
    a = _silu(jnp.dot(x, wg_bf[...], preferred_element_type=F32)) \
        * jnp.dot(x, wu_bf[...], preferred_element_type=F32)
    o_ref[...] = a.astype(BF16)


def _moe_up(x_slots, block_e, w_gate, w_up, layer):
    slots, D = x_slots.shape
    F = w_gate.shape[-1]
    tf = 896
    n_blocks = slots // MOE_BLOCK
    wspec = pl.BlockSpec((None, None, D, tf), lambda f, b, be: (layer, be[b], 0, f))
    return pl.pallas_call(
        _moe_up_kernel,
        out_shape=jax.ShapeDtypeStruct((slots, F), BF16),
        grid_spec=pltpu.PrefetchScalarGridSpec(
            num_scalar_prefetch=1,
            grid=(F // tf, n_blocks),
            in_specs=[pl.BlockSpec((MOE_BLOCK, D), lambda f, b, be: (b, 0)), wspec, wspec],
            out_specs=pl.BlockSpec((MOE_BLOCK, tf), lambda f, b, be: (b, f)),
            scratch_shapes=[pltpu.VMEM((D, tf), BF16), pltpu.VMEM((D, tf), BF16)]),
        compiler_params=_cparams("arbitrary", "arbitrary"),
        name="moe_gate_up",
    )(block_e, x_slots, w_gate, w_up)


def _moe_down_kernel(be_ref, a_ref, wd_ref, o_ref, wd_bf):
    b = pl.program_id(1)
    prev = be_ref[jnp.maximum(b - 1, 0)]

    @pl.when((b == 0) | (be_ref[b] != prev))
    def _():
        wd_bf[...] = wd_ref[...].astype(BF16)

    o_ref[...] = jnp.dot(a_ref[...], wd_bf[...], preferred_element_type=F32)


def _moe_down(a_slots, block_e, w_down, layer):
    slots, F = a_slots.shape
    D = w_down.shape[-1]
    tn = 512
    n_blocks = slots // MOE_BLOCK
    return pl.pallas_call(
        _moe_down_kernel,
        out_shape=jax.ShapeDtypeStruct((slots, D), F32),
        grid_spec=pltpu.PrefetchScalarGridSpec(
            num_scalar_prefetch=1,
            grid=(D // tn, n_blocks),
            in_specs=[pl.BlockSpec((MOE_BLOCK, F), lambda n, b, be: (b, 0)),
                      pl.BlockSpec((None, None, F, tn), lambda n, b, be: (layer, be[b], 0, n))],
            out_specs=pl.BlockSpec((MOE_BLOCK, tn), lambda n, b, be: (b, n)),
            scratch_shapes=[pltpu.VMEM((F, tn), BF16)]),
        compiler_params=_cparams("arbitrary", "arbitrary"),
        name="moe_down",
    )(block_e, a_slots, w_down)


def _moe_combine_kernel(cur_ref, nxt_ref, x_ref, g2_ref, top_ref, fw_ref, y_ref, o_ref, buf_ref, sem, *,
                        tm, final):
    i = pl.program_id(0) * pl.num_programs(1) + pl.program_id(1)
    n = pl.num_programs(0) * pl.num_programs(1)

    def issue(idx_ref, slot):
        def body(r, carry):
            for k in range(TOP_K):
                pltpu.make_async_copy(y_ref.at[pl.ds(idx_ref[0, 0, TOP_K * r + k], 1)],
                                      buf_ref.at[slot, k, pl.ds(r, 1)], sem.at[slot]).start(priority=k)
            return carry
        lax.fori_loop(0, tm, body, 0, unroll=4)

    @pl.when(i == 0)
    def _():
        issue(cur_ref, 0)

    @pl.when(i + 1 < n)
    def _():
        issue(nxt_ref, (i + 1) % 2)

    slot = i % 2
    for k in range(TOP_K):
        pltpu.make_async_copy(y_ref.at[pl.ds(0, tm)], buf_ref.at[slot, k], sem.at[slot]).wait()
    top = top_ref[...]
    f = top[:, 2:3] * buf_ref[slot, 0] + top[:, 3:4] * buf_ref[slot, 1]
    res = x_ref[...] + g2_ref[...] * f
    o_ref[...] = _rms(res) * fw_ref[...] if final else res


def _moe_combine(x, g2, top, dest, y_slots, final_w, final):
    B, S, D = x.shape
    tm = min(S, 512)
    nb = S // tm
    row = lambda b, i: (b, i, 0)
    idx = dest.reshape(B * nb, 1, TOP_K * tm)
    return pl.pallas_call(
        functools.partial(_moe_combine_kernel, tm=tm, final=final),
        out_shape=jax.ShapeDtypeStruct((B, S, D), F32),
        grid=(B, nb),
        in_specs=[pl.BlockSpec((1, 1, TOP_K * tm), lambda b, i: (b * nb + i, 0, 0), memory_space=pltpu.SMEM),
                  pl.BlockSpec((1, 1, TOP_K * tm), lambda b, i: (jnp.minimum(b * nb + i + 1, B * nb - 1), 0, 0),
                               memory_space=pltpu.SMEM),
                  pl.BlockSpec((None, tm, D), row),
                  pl.BlockSpec((None, 1, D), lambda b, i: (b, 0, 0)),
                  pl.BlockSpec((None, tm, LANES), row),
                  pl.BlockSpec((1, D), lambda b, i: (0, 0)),
                  pl.BlockSpec(memory_space=pl.ANY)],
        out_specs=pl.BlockSpec((None, tm, D), row),
        scratch_shapes=[pltpu.VMEM((2, TOP_K, tm, D), F32), pltpu.SemaphoreType.DMA((2,))],
        compiler_params=_cparams("arbitrary", "arbitrary"),
        name="moe_combine",
    )(idx, idx, x, g2, top, final_w.reshape(1, D), y_slots)


def _moe_ffn(x, sc, sh, g2, router, w_gate, w_up, w_down, layer, final_w, final):
    B, S, D = x.shape
    T = B * S
    router_pad = jnp.zeros((D, LANES), F32).at[:, :N_EXPERTS].set(router)
    h, top = _moe_pre(x, sc, sh, router_pad)
    top2d = top.reshape(T, LANES)
    flat_e = top2d[:, :TOP_K].astype(jnp.int32).reshape(-1)
    n_assign = T * TOP_K
    n_blocks = (n_assign + N_EXPERTS * (MOE_BLOCK - 1)) // MOE_BLOCK
    slots = n_blocks * MOE_BLOCK
    onehot = (flat_e[:, None] == jnp.arange(N_EXPERTS, dtype=jnp.int32)[None, :]).astype(jnp.int32)
    csum = jnp.cumsum(onehot, axis=0)
    rank = jnp.sum(csum * onehot, axis=1) - 1
    counts = csum[-1]
    padded = (counts + MOE_BLOCK - 1) // MOE_BLOCK * MOE_BLOCK
    pad_end = jnp.cumsum(padded)
    start_padded = pad_end - padded
    dest = (start_padded[flat_e] + rank).astype(jnp.int32)
    block_start = jnp.arange(n_blocks, dtype=jnp.int32) * MOE_BLOCK
    block_e = jnp.minimum(jnp.sum((block_start[:, None] >= pad_end[None, :]).astype(jnp.int32), axis=1),
                          N_EXPERTS - 1).astype(jnp.int32)
    flat_tok = jnp.repeat(jnp.arange(T, dtype=jnp.int32), TOP_K)
    slot_tok = (jnp.arange(slots, dtype=jnp.int32) % T).at[dest].set(flat_tok)
    x_slots = _moe_gather(h.reshape(T, D), slot_tok)
    a_slots = _moe_up(x_slots, block_e, w_gate, w_up, layer)
    y_slots = _moe_down(a_slots, block_e, w_down, layer)
    return _moe_combine(x, g2, top, dest, y_slots, final_w, final)


def _final_norm_kernel(x_ref, w_ref, o_ref):
    o_ref[...] = _rms(x_ref[...]) * w_ref[...]


def _final_norm(x2d, w):
    T, D = x2d.shape
    tm = min(T, 1024)
    return pl.pallas_call(
        _final_norm_kernel,
        out_shape=jax.ShapeDtypeStruct((T, D), F32),
        grid=(T // tm,),
        in_specs=[pl.BlockSpec((tm, D), lambda i: (i, 0)), pl.BlockSpec((1, D), lambda i: (0, 0))],
        out_specs=pl.BlockSpec((tm, D), lambda i: (i, 0)),
        compiler_params=_cparams("parallel"),
        name="final_norm",
    )(x2d, w.reshape(1, D))


def _arrange_w_in(w):
    D = w.shape[0]
    o = np.cumsum([0, 3 * GDN_DIM, GDN_DIM, GDN_HEADS, GDN_HEADS, SSM_INNER, SSM_CONV_DIM, SSM_HEADS,
                   MLA_Q_LORA, MLA_KV_LORA, MLA_ROPE, 3 * w.shape[0]])
    seg = lambda i: w[:, int(o[i]):int(o[i + 1])]
    qkv, gz, ga, gb, sz, xbc, dt, cq, ckv, kr, gates = [seg(i) for i in range(11)]
    half = MLA_ROPE // 2
    small = jnp.concatenate([ga, gb, dt, jnp.zeros((D, LANES - 2 * GDN_HEADS - SSM_HEADS), w.dtype)], axis=1)
    kr_full = jnp.concatenate([kr, -kr[:, half:], kr[:, :half]], axis=1)
    out = jnp.concatenate([qkv, gz, sz, cq, xbc, gates, ckv, small, kr_full], axis=1)
    assert out.shape[1] == IN_COLS
    return out.astype(BF16)


def _arrange_w_uq(w):
    half = MLA_ROPE // 2
    parts = []
    for h in range(MLA_HEADS):
        c0 = h * (MLA_NOPE + MLA_ROPE)
        rope = w[:, c0 + MLA_NOPE:c0 + MLA_NOPE + MLA_ROPE]
        parts += [w[:, c0:c0 + MLA_NOPE], rope, -rope[:, half:], rope[:, :half]]
    return jnp.concatenate(parts, axis=1).astype(BF16)


def kernel(x, c, positions, w_ada, b_ada, w_in, gdn_conv_w, gdn_a_log, gdn_dt_bias, gdn_norm_w, ssm_conv_w, ssm_conv_b, ssm_a_log, ssm_dt_bias, ssm_d, ssm_norm_w, mla_q_norm_w, mla_w_uq, mla_kv_norm_w, mla_w_uk, mla_w_uv, w_branch_a, w_branch_b, w_branch_c, w_out, ffn_w_gate, ffn_w_up, ffn_w_down, moe_router, moe_w_gate, moe_w_up, moe_w_down, final_norm_w):
    B, S, D = x.shape
    T = B * S
    depth = w_ada.shape[0]
    c_pad = jnp.zeros((SUBLANES, D), F32).at[:B].set(c)
    mod = _modulation(c_pad, w_ada, b_ada)[:, :B].reshape(depth, B, 6, 1, D)
    cos, sin = _rope_tables(positions)
    for l in range(depth):
        sh1, sc1, g1, sh2, sc2, g2 = [mod[l, :, n] for n in range(6)]
        proj, small = _in_proj(x, sc1, sh1, _arrange_w_in(w_in[l]))
        y_a = _gdn(proj, small, gdn_conv_w[l], gdn_a_log[l], gdn_dt_bias[l], gdn_norm_w[l])
        y_b = _ssd(proj, small, ssm_conv_w[l], ssm_conv_b[l], ssm_a_log[l], ssm_dt_bias[l], ssm_d[l], ssm_norm_w[l])
        q, k, v = _mla_proj(proj.reshape(T, IN_COLS), cos, sin, mla_q_norm_w[l], _arrange_w_uq(mla_w_uq[l]),
                            mla_kv_norm_w[l], mla_w_uk[l].astype(BF16), mla_w_uv[l].astype(BF16))
        y_c = _attention(q.reshape(B, S, -1), k.reshape(B, S, -1), v.reshape(B, S, -1))
        x = _merge(y_a, y_b, y_c, proj, x, g1, w_branch_a[l].astype(BF16), w_branch_b[l].astype(BF16),
                   w_branch_c[l].astype(BF16), w_out[l].astype(BF16))
        i = l // 2
        if l % 2 == 0:
            x = _dense_ffn(x, sc2, sh2, g2, ffn_w_gate, ffn_w_up, ffn_w_down, i)
        else:
            x = _moe_ffn(x, sc2, sh2, g2, moe_router[i], moe_w_gate, moe_w_up, moe_w_down, i,
                         final_norm_w, l == depth - 1)
    if depth % 2 == 0:
        return x
    return _final_norm(x.reshape(T, D), final_norm_w).reshape(B, S, D)
```

```python
import functools
import math

import numpy as np
import jax
import jax.numpy as jnp
from jax import lax
from jax.experimental import pallas as pl
from jax.experimental.pallas import tpu as pltpu

F32 = jnp.float32
BF16 = jnp.bfloat16

EPS = 1e-6
CONV_K = 4
GDN_HEADS = 4
GDN_HEAD_DIM = 128
GDN_DIM = GDN_HEADS * GDN_HEAD_DIM
SSM_HEADS = 8
SSM_HEAD_DIM = 64
SSM_INNER = SSM_HEADS * SSM_HEAD_DIM
SSM_GROUPS = 2
SSM_STATE = 128
SSM_CONV_DIM = SSM_INNER + 2 * SSM_GROUPS * SSM_STATE
MLA_HEADS = 4
MLA_Q_LORA = 512
MLA_KV_LORA = 256
MLA_NOPE = 128
MLA_ROPE = 64
MLA_V = 128
MLA_DIM = MLA_HEADS * MLA_V
ROPE_THETA = 10000.0
N_EXPERTS = 8
TOP_K = 2
MOE_BLOCK = 1024

LANES = 128
SUBLANES = 8
VMEM_LIMIT = 56 * 1024 * 1024

COL_QKV = 0
COL_GZ = 1536
COL_SZ = 2048
COL_CQ = 2560
COL_XBC = 3072
COL_GATES = 4096
COL_CKV = 7168
COL_SMALL = 7424
COL_KR = 7552
IN_COLS = 7680


def _cparams(*sem):
    return pltpu.CompilerParams(dimension_semantics=tuple(sem), vmem_limit_bytes=VMEM_LIMIT)


def _silu(x):
    return x * (1.0 / (1.0 + jnp.exp(-x)))


def _sigmoid(x):
    return 1.0 / (1.0 + jnp.exp(-x))


def _softplus(x):
    return jnp.maximum(x, 0.0) + jnp.log(1.0 + jnp.exp(-jnp.abs(x)))


def _rms(x):
    return x * lax.rsqrt(jnp.mean(x * x, axis=-1, keepdims=True) + EPS)


def _dot(a, b):
    return jnp.dot(a.astype(BF16), b.astype(BF16), preferred_element_type=F32)


def _dot_nt(a, b):
    return lax.dot_general(a.astype(BF16), b.astype(BF16), (((1,), (1,)), ((), ())),
                           preferred_element_type=F32)


def _dot_tn(a, b):
    return lax.dot_general(a.astype(BF16), b.astype(BF16), (((0,), (0,)), ((), ())),
                           preferred_element_type=F32)


def _dot_hi(a, b):
    a_hi, b_hi = a.astype(BF16), b.astype(BF16)
    a_lo = (a - a_hi.astype(F32)).astype(BF16)
    b_lo = (b - b_hi.astype(F32)).astype(BF16)
    d = lambda x, y: jnp.dot(x, y, preferred_element_type=F32)
    return d(a_hi, b_hi) + (d(a_hi, b_lo) + d(a_lo, b_hi))


def _mod_kernel(c_ref, w_ref, b_ref, o_ref):
    cond = _silu(c_ref[...])
    o_ref[...] = _dot(cond, w_ref[...]) + b_ref[...]


def _modulation(c_pad, w_ada, b_ada):
    L, D, N = w_ada.shape
    tn = 1536
    return pl.pallas_call(
        _mod_kernel,
        out_shape=jax.ShapeDtypeStruct((L, SUBLANES, N), F32),
        grid=(L, N // tn),
        in_specs=[pl.BlockSpec((SUBLANES, D), lambda l, j: (0, 0)),
                  pl.BlockSpec((None, D, tn), lambda l, j: (l, 0, j)),
                  pl.BlockSpec((None, 1, tn), lambda l, j: (l, 0, j))],
        out_specs=pl.BlockSpec((None, SUBLANES, tn), lambda l, j: (l, 0, j)),
        compiler_params=_cparams("parallel", "parallel"),
        name="adaln_mod",
    )(c_pad, w_ada, b_ada.reshape(L, 1, N))


def _rope_table_kernel(pos_ref, freq_ref, cos_ref, sin_ref):
    ang = pos_ref[...].astype(F32) * freq_ref[...]
    keep = lax.broadcasted_iota(jnp.int32, ang.shape, 1) < MLA_ROPE
    cos_ref[...] = jnp.where(keep, jnp.cos(ang), 0.0)
    sin_ref[...] = jnp.where(keep, jnp.sin(ang), 0.0)


def _rope_tables(positions):
    T = positions.size
    tm = min(T, 1024)
    inv_freq = ROPE_THETA ** (-jnp.arange(0, MLA_ROPE, 2, dtype=F32) / MLA_ROPE)
    freq = jnp.concatenate([inv_freq, inv_freq, jnp.zeros((LANES - MLA_ROPE,), F32)]).reshape(1, LANES)
    return pl.pallas_call(
        _rope_table_kernel,
        out_shape=(jax.ShapeDtypeStruct((T, LANES), F32), jax.ShapeDtypeStruct((T, LANES), F32)),
        grid=(T // tm,),
        in_specs=[pl.BlockSpec((tm, 1), lambda i: (i, 0)),
                  pl.BlockSpec((1, LANES), lambda i: (0, 0))],
        out_specs=(pl.BlockSpec((tm, LANES), lambda i: (i, 0)),
                   pl.BlockSpec((tm, LANES), lambda i: (i, 0))),
        compiler_params=_cparams("parallel"),
        name="rope_tables",
    )(positions.reshape(T, 1), freq)


IN_TN = 2560
SMALL_TILE = COL_SMALL // IN_TN
SMALL_OFF = COL_SMALL % IN_TN


def _in_proj_kernel(x_ref, sc_ref, sh_ref, w_ref, o_ref, small_ref, h_ref):
    j = pl.program_id(2)

    @pl.when(j == 0)
    def _():
        h = _rms(x_ref[...]) * (1.0 + sc_ref[...]) + sh_ref[...]
        h_ref[...] = h.astype(BF16)

    o = jnp.dot(h_ref[...], w_ref[...], preferred_element_type=F32)
    o_ref[...] = o.astype(BF16)

    @pl.when(j == SMALL_TILE)
    def _():
        small_ref[...] = o[:, SMALL_OFF:SMALL_OFF + LANES]


def _in_proj(x, sc, sh, w):
    B, S, D = x.shape
    N = w.shape[1]
    tm = min(S, 1024)
    tn = IN_TN
    return pl.pallas_call(
        _in_proj_kernel,
        out_shape=(jax.ShapeDtypeStruct((B, S, N), BF16), jax.ShapeDtypeStruct((B, S, LANES), F32)),
        grid=(B, S // tm, N // tn),
        in_specs=[pl.BlockSpec((None, tm, D), lambda b, i, j: (b, i, 0)),
                  pl.BlockSpec((None, 1, D), lambda b, i, j: (b, 0, 0)),
                  pl.BlockSpec((None, 1, D), lambda b, i, j: (b, 0, 0)),
                  pl.BlockSpec((D, tn), lambda b, i, j: (0, j))],
        out_specs=(pl.BlockSpec((None, tm, tn), lambda b, i, j: (b, i, j)),
                   pl.BlockSpec((None, tm, LANES), lambda b, i, j: (b, i, 0))),
        scratch_shapes=[pltpu.VMEM((tm, D), BF16)],
        compiler_params=_cparams("parallel", "parallel", "arbitrary"),
        name="in_proj",
    )(x, sc, sh, w)


def _causal_conv(x, tail_ref, w):
    rows = x.shape[0]
    xe = jnp.concatenate([tail_ref[...], x], axis=0)
    tail_ref[...] = x[rows - SUBLANES:, :]
    y = xe[SUBLANES:, :] * w[CONV_K - 1:CONV_K, :]
    for j in range(CONV_K - 1):
        shift = CONV_K - 1 - j
        y = y + pltpu.roll(xe, shift, axis=0)[SUBLANES:, :] * w[j:j + 1, :]
    return y


def _chunk_cumsum(v, chunk):
    pos = lax.broadcasted_iota(jnp.int32, v.shape, 0) % chunk
    step = 1
    while step < chunk:
        v = v + jnp.where(pos >= step, pltpu.roll(v, step, axis=0), 0.0)
        step *= 2
    return v


def _col_to_row(col_b, eye):
    return jnp.sum(jnp.where(eye, col_b, 0.0), axis=0, keepdims=True)


GDN_CHUNK = 128


def _bf16(a):
    return a.astype(BF16)


def _mm(a, b):
    return jnp.dot(a, b, preferred_element_type=F32)


def _gdn_kernel(qkv_ref, z_ref, sm_ref, cw_ref, par_ref, nw_ref, o_ref,
                tail_ref, state_ref, *, rows):
    C, H, Dh = GDN_CHUNK, GDN_HEADS, GDN_HEAD_DIM
    n_ck = rows // C

    @pl.when(pl.program_id(1) == 0)
    def _():
        tail_ref[...] = jnp.zeros_like(tail_ref)
        state_ref[...] = jnp.zeros_like(state_ref)

    y = _silu(_causal_conv(qkv_ref[...].astype(F32), tail_ref, cw_ref[...]))
    sm = sm_ref[...]
    neg_a = par_ref[0:1, :]
    dt_b = par_ref[1:2, :]
    g_all = _chunk_cumsum(neg_a * _softplus(sm + dt_b), C)
    beta_all = _sigmoid(sm)
    z = z_ref[...].astype(F32)
    nw = nw_ref[...]

    ri = lax.broadcasted_iota(jnp.int32, (C, C), 0)
    ci = lax.broadcasted_iota(jnp.int32, (C, C), 1)
    eye = ri == ci
    incl = ri >= ci
    strict = ri > ci
    eye_f = jnp.where(eye, 1.0, 0.0)

    qn, kn, vn = [], [], []
    for h in range(H):
        q_h = y[:, h * Dh:(h + 1) * Dh]
        k_h = y[:, GDN_DIM + h * Dh:GDN_DIM + (h + 1) * Dh]
        qn.append(q_h * lax.rsqrt(jnp.sum(q_h * q_h, axis=-1, keepdims=True) + EPS) * (Dh ** -0.5))
        kn.append(k_h * lax.rsqrt(jnp.sum(k_h * k_h, axis=-1, keepdims=True) + EPS))
        vn.append(y[:, 2 * GDN_DIM + h * Dh:2 * GDN_DIM + (h + 1) * Dh])

    items = [(c, h) for c in range(n_ck) for h in range(H)]
    p, rhs, qk, q_dec, k_dec, g_end = {}, {}, {}, {}, {}, {}
    for c, h in items:
        r0 = c * C
        q_c, k_c, v_c = qn[h][r0:r0 + C], kn[h][r0:r0 + C], vn[h][r0:r0 + C]
        g_col = g_all[r0:r0 + C, h:h + 1]
        beta = beta_all[r0:r0 + C, H + h:H + h + 1]
        g_b = jnp.broadcast_to(g_col, (C, C))
        g_last = g_col[C - 1:C, :]
        decay = jnp.exp(jnp.where(incl, g_b - _col_to_row(g_b, eye), -1e30))
        eg = jnp.exp(g_col)
        p[c, h] = jnp.where(strict, -(beta * _dot_nt(k_c, k_c) * decay), 0.0)
        qk[c, h] = (_dot_nt(q_c, k_c) * decay).astype(BF16)
        rhs[c, h] = _bf16(jnp.concatenate([v_c * beta, k_c * (beta * eg)], axis=-1))
        q_dec[c, h] = (q_c * eg).astype(BF16)
        k_dec[c, h] = (k_c * jnp.exp(g_last - g_col)).astype(BF16)
        g_end[c, h] = jnp.exp(g_last)

    inv = {it: eye_f + jnp.where((ri // 2 == ci // 2) & strict, p[it], 0.0) for it in items}
    s = 2
    while s < C:
        level = ((ri // s) % 2 == 1) & ((ci // s) % 2 == 0) & (ri // (2 * s) == ci // (2 * s))
        t_bf = {it: _bf16(inv[it]) for it in items}
        x = {it: _mm(_bf16(jnp.where(level, p[it], 0.0)), t_bf[it]) for it in items}
        inv = {it: inv[it] + _mm(t_bf[it], _bf16(x[it])) for it in items}
        s *= 2
    sol = {it: _mm(_bf16(inv[it]), rhs[it]) for it in items}

    states = [state_ref[h] for h in range(H)]
    for c in range(n_ck):
        r0 = c * C
        s_bf = [s.astype(BF16) for s in states]
        v_new = [sol[c, h][:, :Dh] - _dot(sol[c, h][:, Dh:], s_bf[h]) for h in range(H)]
        v_bf = [v.astype(BF16) for v in v_new]
        outs = [jnp.dot(q_dec[c, h], s_bf[h], preferred_element_type=F32)
                + jnp.dot(qk[c, h], v_bf[h], preferred_element_type=F32) for h in range(H)]
        states = [states[h] * g_end[c, h]
                  + lax.dot_general(k_dec[c, h], v_bf[h], (((0,), (0,)), ((), ())), preferred_element_type=F32)
                  for h in range(H)]
        for h in range(H):
            z_c = z[r0:r0 + C, h * Dh:(h + 1) * Dh]
            o_ref[r0:r0 + C, h * Dh:(h + 1) * Dh] = (_rms(outs[h]) * nw * _silu(z_c)).astype(BF16)
    for h in range(H):
        state_ref[h] = states[h]


def _gdn(proj, small, conv_w, a_log, dt_bias, norm_w):
    B, S, _ = proj.shape
    rows = min(S, 512)
    par = jnp.zeros((SUBLANES, LANES), F32)
    par = par.at[0, :GDN_HEADS].set(-jnp.exp(a_log)).at[1, :GDN_HEADS].set(dt_bias)
    w3 = 3 * GDN_DIM
    return pl.pallas_call(
        functools.partial(_gdn_kernel, rows=rows),
        out_shape=jax.ShapeDtypeStruct((B, S, GDN_DIM), BF16),
        grid=(B, S // rows),
        in_specs=[pl.BlockSpec((None, rows, w3), lambda b, i: (b, i, COL_QKV // w3)),
                  pl.BlockSpec((None, rows, GDN_DIM), lambda b, i: (b, i, COL_GZ // GDN_DIM)),
                  pl.BlockSpec((None, rows, LANES), lambda b, i: (b, i, 0)),
                  pl.BlockSpec((CONV_K, w3), lambda b, i: (0, 0)),
                  pl.BlockSpec((SUBLANES, LANES), lambda b, i: (0, 0)),
                  pl.BlockSpec((1, GDN_HEAD_DIM), lambda b, i: (0, 0))],
        out_specs=pl.BlockSpec((None, rows, GDN_DIM), lambda b, i: (b, i, 0)),
        scratch_shapes=[pltpu.VMEM((SUBLANES, w3), F32),
                        pltpu.VMEM((GDN_HEADS, GDN_HEAD_DIM, GDN_HEAD_DIM), F32)],
        compiler_params=_cparams("parallel", "arbitrary"),
        name="gated_delta_net",
    )(proj, proj, small, conv_w, par, norm_w.reshape(1, GDN_HEAD_DIM))


SSD_CHUNK = 128


def _ssd_kernel(xbc_ref, z_ref, sm_ref, cw_ref, cb_ref, par_ref, nw_ref, o_ref,
                tail_ref, state_ref, *, rows):
    C, H, P, G, N = SSD_CHUNK, SSM_HEADS, SSM_HEAD_DIM, SSM_GROUPS, SSM_STATE
    E = H // G
    GW = E * P

    @pl.when(pl.program_id(1) == 0)
    def _():
        tail_ref[...] = jnp.zeros_like(tail_ref)
        state_ref[...] = jnp.zeros_like(state_ref)

    y = _silu(_causal_conv(xbc_ref[...].astype(F32), tail_ref, cw_ref[...]) + cb_ref[...])
    sm = sm_ref[...]
    neg_a = par_ref[0:1, :]
    dt_b = par_ref[1:2, :]
    d_skip = par_ref[2:3, :]
    dt_all = _softplus(sm + dt_b)
    acs_all = _chunk_cumsum(dt_all * neg_a, C)
    z = z_ref[...].astype(F32)
    nw = nw_ref[...]

    ri = lax.broadcasted_iota(jnp.int32, (C, C), 0)
    ci = lax.broadcasted_iota(jnp.int32, (C, C), 1)
    eye = ri == ci
    incl = ri >= ci

    items = [(c, g) for c in range(rows // C) for g in range(G)]
    cm, y_local, st_add, e_in, e_out = {}, {}, {}, {}, {}
    for c, g in items:
        r0 = c * C
        bm_c = y[r0:r0 + C, SSM_INNER + g * N:SSM_INNER + (g + 1) * N]
        cm_c = y[r0:r0 + C, SSM_INNER + G * N + g * N:SSM_INNER + G * N + (g + 1) * N]
        x_c = y[r0:r0 + C, g * GW:(g + 1) * GW]
        cb = _dot_nt(cm_c, bm_c)
        xs_parts, ea_parts, ydiag_parts, dx_parts, last_parts = [], [], [], [], []
        for e in range(E):
            lane = SUBLANES + g * E + e
            a_col = acs_all[r0:r0 + C, lane:lane + 1]
            dt_col = dt_all[r0:r0 + C, lane:lane + 1]
            a_b = jnp.broadcast_to(a_col, (C, C))
            a_last = a_col[C - 1:C, :]
            L = jnp.exp(jnp.where(incl, a_b - _col_to_row(a_b, eye), -1e30))
            x_e = x_c[:, e * P:(e + 1) * P]
            xd = x_e * dt_col
            ydiag_parts.append(_dot(cb * L, xd))
            xs_parts.append(xd * jnp.exp(a_last - a_col))
            ea_parts.append(jnp.broadcast_to(jnp.exp(a_col), (C, P)))
            last_parts.append(jnp.broadcast_to(jnp.exp(a_last), (1, P)))
            dx_parts.append(x_e * d_skip[:, lane:lane + 1])
        cm[c, g] = cm_c.astype(BF16)
        y_local[c, g] = jnp.concatenate(ydiag_parts, axis=-1) + jnp.concatenate(dx_parts, axis=-1)
        st_add[c, g] = _dot_tn(bm_c, jnp.concatenate(xs_parts, axis=-1))
        e_in[c, g] = jnp.concatenate(ea_parts, axis=-1)
        e_out[c, g] = jnp.concatenate(last_parts, axis=-1)

    states = [state_ref[g] for g in range(G)]
    for c in range(rows // C):
        r0 = c * C
        for g in range(G):
            y_off = jnp.dot(cm[c, g], states[g].astype(BF16), preferred_element_type=F32)
            y_c = (y_local[c, g] + y_off * e_in[c, g]) * _silu(z[r0:r0 + C, g * GW:(g + 1) * GW])
            o_ref[r0:r0 + C, g * GW:(g + 1) * GW] = (_rms(y_c) * nw[:, g * GW:(g + 1) * GW]).astype(BF16)
            states[g] = states[g] * e_out[c, g] + st_add[c, g]
    for g in range(G):
        state_ref[g] = states[g]


def _ssd(proj, small, conv_w, conv_b, a_log, dt_bias, d_skip, norm_w):
    B, S, _ = proj.shape
    rows = min(S, 512)
    par = jnp.zeros((SUBLANES, LANES), F32)
    sl = slice(SUBLANES, SUBLANES + SSM_HEADS)
    par = par.at[0, sl].set(-jnp.exp(a_log)).at[1, sl].set(dt_bias).at[2, sl].set(d_skip)
    return pl.pallas_call(
        functools.partial(_ssd_kernel, rows=rows),
        out_shape=jax.ShapeDtypeStruct((B, S, SSM_INNER), BF16),
        grid=(B, S // rows),
        in_specs=[pl.BlockSpec((None, rows, SSM_CONV_DIM), lambda b, i: (b, i, COL_XBC // SSM_CONV_DIM)),
                  pl.BlockSpec((None, rows, SSM_INNER), lambda b, i: (b, i, COL_SZ // SSM_INNER)),
                  pl.BlockSpec((None, rows, LANES), lambda b, i: (b, i, 0)),
                  pl.BlockSpec((CONV_K, SSM_CONV_DIM), lambda b, i: (0, 0)),
                  pl.BlockSpec((1, SSM_CONV_DIM), lambda b, i: (0, 0)),
                  pl.BlockSpec((SUBLANES, LANES), lambda b, i: (0, 0)),
                  pl.BlockSpec((1, SSM_INNER), lambda b, i: (0, 0))],
        out_specs=pl.BlockSpec((None, rows, SSM_INNER), lambda b, i: (b, i, 0)),
        scratch_shapes=[pltpu.VMEM((SUBLANES, SSM_CONV_DIM), F32),
                        pltpu.VMEM((SSM_GROUPS, SSM_STATE, SSM_INNER // SSM_GROUPS), F32)],
        compiler_params=_cparams("parallel", "arbitrary"),
        name="mamba2_ssd",
    )(proj, proj, small, conv_w, conv_b.reshape(1, -1), par, norm_w.reshape(1, -1))


QK_W = 2 * LANES


def _rope(x, cos, sin):
    return x * cos + pltpu.roll(x, MLA_ROPE, axis=1) * sin


def _mla_proj_kernel(cq_ref, ckv_ref, kr_ref, cos_ref, sin_ref, qw_ref, wq_ref, kvw_ref, wk_ref, wv_ref,
                     q_ref, k_ref, v_ref):
    scale = (MLA_NOPE + MLA_ROPE) ** -0.5 * math.log2(math.e)
    cos, sin = cos_ref[...], sin_ref[...]
    qc = _rms(cq_ref[...].astype(F32)) * qw_ref[...]
    qf = _dot(qc, wq_ref[...]) * scale
    ckv = (_rms(ckv_ref[...].astype(F32)) * kvw_ref[...]).astype(BF16)
    kn = jnp.dot(ckv, wk_ref[...], preferred_element_type=F32)
    v_ref[...] = jnp.dot(ckv, wv_ref[...], preferred_element_type=F32).astype(BF16)
    kr = _rope(kr_ref[...].astype(F32), cos, sin).astype(BF16)
    for h in range(MLA_HEADS):
        c0 = h * QK_W
        q_ref[:, c0:c0 + LANES] = qf[:, c0:c0 + LANES].astype(BF16)
        q_ref[:, c0 + LANES:c0 + QK_W] = _rope(qf[:, c0 + LANES:c0 + QK_W], cos, sin).astype(BF16)
        k_ref[:, c0:c0 + LANES] = kn[:, h * MLA_NOPE:(h + 1) * MLA_NOPE].astype(BF16)
        k_ref[:, c0 + LANES:c0 + QK_W] = kr


def _mla_proj(proj2d, cos, sin, q_norm_w, wq, kv_norm_w, wk, wv):
    T = proj2d.shape[0]
    tm = min(T, 512)
    HQ = MLA_HEADS * QK_W
    row = lambda i: (i, 0)
    const = lambda i: (0, 0)
    return pl.pallas_call(
        _mla_proj_kernel,
        out_shape=(jax.ShapeDtypeStruct((T, HQ), BF16), jax.ShapeDtypeStruct((T, HQ), BF16),
                   jax.ShapeDtypeStruct((T, MLA_DIM), BF16)),
        grid=(T // tm,),
        in_specs=[pl.BlockSpec((tm, MLA_Q_LORA), lambda i: (i, COL_CQ // MLA_Q_LORA)),
                  pl.BlockSpec((tm, MLA_KV_LORA), lambda i: (i, COL_CKV // MLA_KV_LORA)),
                  pl.BlockSpec((tm, LANES), lambda i: (i, COL_KR // LANES)),
                  pl.BlockSpec((tm, LANES), row),
                  pl.BlockSpec((tm, LANES), row),
                  pl.BlockSpec((1, MLA_Q_LORA), const),
                  pl.BlockSpec((MLA_Q_LORA, HQ), const),
                  pl.BlockSpec((1, MLA_KV_LORA), const),
                  pl.BlockSpec((MLA_KV_LORA, MLA_HEADS * MLA_NOPE), const),
                  pl.BlockSpec((MLA_KV_LORA, MLA_DIM), const)],
        out_specs=(pl.BlockSpec((tm, HQ), row), pl.BlockSpec((tm, HQ), row),
                   pl.BlockSpec((tm, MLA_DIM), row)),
        compiler_params=_cparams("parallel"),
        name="mla_proj",
    )(proj2d, proj2d, proj2d, cos, sin, q_norm_w.reshape(1, -1), wq, kv_norm_w.reshape(1, -1), wk, wv)


def _attn_kernel(q_ref, k_ref, v_ref, o_ref, m_ref, l_ref, acc_ref, *, t):
    qi = pl.program_id(1)
    m_ref[...] = jnp.full_like(m_ref, -1e30)
    l_ref[...] = jnp.zeros_like(l_ref)
    acc_ref[...] = jnp.zeros_like(acc_ref)

    def block(j, diagonal):
        k0 = pl.multiple_of(j * t, t)
        hs = range(MLA_HEADS)
        s = [lax.dot_general(q_ref[:, h * QK_W:(h + 1) * QK_W], k_ref[pl.ds(k0, t), h * QK_W:(h + 1) * QK_W],
                             (((1,), (1,)), ((), ())), preferred_element_type=F32) for h in hs]
        if diagonal:
            row = lax.broadcasted_iota(jnp.int32, s[0].shape, 0)
            col = lax.broadcasted_iota(jnp.int32, s[0].shape, 1)
            s = [jnp.where(row >= col, s[h], -1e30) for h in hs]
        cols = range(t // LANES)
        sc = [[s[h][:, c * LANES:(c + 1) * LANES] for c in cols] for h in hs]
        m_prev = [m_ref[h] for h in hs]
        m_new = [jnp.maximum(m_prev[h], jnp.max(functools.reduce(jnp.maximum, sc[h]), axis=-1, keepdims=True))
                 for h in hs]
        p = [[jnp.exp2(sc[h][c] - m_new[h]) for c in cols] for h in hs]
        alpha = [jnp.exp2(m_prev[h] - m_new[h]) for h in hs]
        pv = [jnp.dot(jnp.concatenate(p[h], axis=-1).astype(BF16), v_ref[pl.ds(k0, t), h * MLA_V:(h + 1) * MLA_V],
                      preferred_element_type=F32) for h in hs]
        for h in hs:
            l_ref[h] = alpha[h] * l_ref[h] + functools.reduce(jnp.add, p[h])
            acc_ref[h] = alpha[h] * acc_ref[h] + pv[h]
            m_ref[h] = m_new[h]

    def body(j, carry):
        block(j, False)
        return carry

    lax.fori_loop(0, qi, body, 0)
    block(qi, True)
    for h in range(MLA_HEADS):
        o_ref[:, h * MLA_V:(h + 1) * MLA_V] = (acc_ref[h] / jnp.sum(l_ref[h], axis=-1, keepdims=True)).astype(BF16)


def _attention(q, k, v):
    B, S, _ = q.shape
    t = min(S, 512)
    H = MLA_HEADS
    return pl.pallas_call(
        functools.partial(_attn_kernel, t=t),
        out_shape=jax.ShapeDtypeStruct((B, S, MLA_DIM), BF16),
        grid=(B, S // t),
        in_specs=[pl.BlockSpec((None, t, H * QK_W), lambda b, qi: (b, qi, 0)),
                  pl.BlockSpec((None, S, H * QK_W), lambda b, qi: (b, 0, 0)),
                  pl.BlockSpec((None, S, MLA_DIM), lambda b, qi: (b, 0, 0))],
        out_specs=pl.BlockSpec((None, t, MLA_DIM), lambda b, qi: (b, qi, 0)),
        scratch_shapes=[pltpu.VMEM((H, t, LANES), F32), pltpu.VMEM((H, t, LANES), F32),
                        pltpu.VMEM((H, t, MLA_V), F32)],
        compiler_params=_cparams("parallel", "arbitrary"),
        name="mla_attention",
    )(q, k, v)


def _merge_kernel(ya_ref, yb_ref, yc_ref, ga_ref, gb_ref, gc_ref, x_ref, g1_ref,
                  pa_ref, pb_ref, pc_ref, wo_ref, o_ref):
    m = (_sigmoid(ga_ref[...].astype(F32)) * _dot(ya_ref[...], pa_ref[...])
         + _sigmoid(gb_ref[...].astype(F32)) * _dot(yb_ref[...], pb_ref[...])
         + _sigmoid(gc_ref[...].astype(F32)) * _dot(yc_ref[...], pc_ref[...]))
    o_ref[...] = x_ref[...] + g1_ref[...] * _dot(m, wo_ref[...])


def _merge(ya, yb, yc, proj, x, g1, pa, pb, pc, wo):
    B, S, D = x.shape
    tm = min(S, 512)
    W = ya.shape[-1]
    row = lambda b, i: (b, i, 0)
    const = lambda b, i: (0, 0)
    gate = lambda n: pl.BlockSpec((None, tm, D), lambda b, i: (b, i, COL_GATES // D + n))
    return pl.pallas_call(
        _merge_kernel,
        out_shape=jax.ShapeDtypeStruct((B, S, D), F32),
        grid=(B, S // tm),
        in_specs=[pl.BlockSpec((None, tm, W), row), pl.BlockSpec((None, tm, W), row),
                  pl.BlockSpec((None, tm, W), row), gate(0), gate(1), gate(2),
                  pl.BlockSpec((None, tm, D), row),
                  pl.BlockSpec((None, 1, D), lambda b, i: (b, 0, 0)),
                  pl.BlockSpec((W, D), const), pl.BlockSpec((W, D), const), pl.BlockSpec((W, D), const),
                  pl.BlockSpec((D, D), const)],
        out_specs=pl.BlockSpec((None, tm, D), row),
        compiler_params=_cparams("parallel", "parallel"),
        name="branch_merge",
    )(ya, yb, yc, proj, proj, proj, x, g1, pa, pb, pc, wo)


def _ffn_kernel(x_ref, sc_ref, sh_ref, g2_ref, wg_ref, wu_ref, wd_ref, o_ref, h_ref, acc_ref):
    f = pl.program_id(2)

    @pl.when(f == 0)
    def _():
        h = _rms(x_ref[...]) * (1.0 + sc_ref[...]) + sh_ref[...]
        h_ref[...] = h.astype(BF16)
        acc_ref[...] = jnp.zeros_like(acc_ref)

    h = h_ref[...]
    a = _silu(_dot(h, wg_ref[...])) * _dot(h, wu_ref[...])
    acc_ref[...] += _dot(a, wd_ref[...])

    @pl.when(f == pl.num_programs(2) - 1)
    def _():
        o_ref[...] = x_ref[...] + g2_ref[...] * acc_ref[...]


def _dense_ffn(x, sc, sh, g2, wg, wu, wd, layer):
    B, S, D = x.shape
    F = wg.shape[-1]
    tm = min(S, 1024)
    tf = 256
    row = lambda b, i, f: (b, i, 0)
    vec = lambda b, i, f: (b, 0, 0)
    return pl.pallas_call(
        _ffn_kernel,
        out_shape=jax.ShapeDtypeStruct((B, S, D), F32),
        grid=(B, S // tm, F // tf),
        in_specs=[pl.BlockSpec((None, tm, D), row),
                  pl.BlockSpec((None, 1, D), vec), pl.BlockSpec((None, 1, D), vec),
                  pl.BlockSpec((None, 1, D), vec),
                  pl.BlockSpec((None, D, tf), lambda b, i, f: (layer, 0, f)),
                  pl.BlockSpec((None, D, tf), lambda b, i, f: (layer, 0, f)),
                  pl.BlockSpec((None, tf, D), lambda b, i, f: (layer, f, 0))],
        out_specs=pl.BlockSpec((None, tm, D), row),
        scratch_shapes=[pltpu.VMEM((tm, D), BF16), pltpu.VMEM((tm, D), F32)],
        compiler_params=_cparams("parallel", "parallel", "arbitrary"),
        name="dense_ffn",
    )(x, sc, sh, g2, wg, wu, wd)


def _moe_pre_kernel(x_ref, sc_ref, sh_ref, r_ref, h_ref, top_ref):
    h = _rms(x_ref[...]) * (1.0 + sc_ref[...]) + sh_ref[...]
    h_ref[...] = h
    logits = _dot_hi(h, r_ref[...])
    lane = lax.broadcasted_iota(jnp.int32, logits.shape, 1)
    lane_f = lane.astype(F32)
    logits = jnp.where(lane < N_EXPERTS, logits, -jnp.inf)
    m1 = jnp.max(logits, axis=-1, keepdims=True)
    i1 = jnp.min(jnp.where(logits == m1, lane_f, float(LANES)), axis=-1, keepdims=True)
    rest = jnp.where(lane_f == i1, -jnp.inf, logits)
    m2 = jnp.max(rest, axis=-1, keepdims=True)
    i2 = jnp.min(jnp.where(rest == m2, lane_f, float(LANES)), axis=-1, keepdims=True)
    e2 = jnp.exp(m2 - m1)
    w1 = 1.0 / (1.0 + e2)
    w2 = e2 / (1.0 + e2)
    top_ref[...] = jnp.where(lane == 0, i1, jnp.where(lane == 1, i2,
                             jnp.where(lane == 2, w1, jnp.where(lane == 3, w2, 0.0))))


def _moe_pre(x, sc, sh, router_pad):
    B, S, D = x.shape
    tm = min(S, 512)
    row = lambda b, i: (b, i, 0)
    vec = lambda b, i: (b, 0, 0)
    return pl.pallas_call(
        _moe_pre_kernel,
        out_shape=(jax.ShapeDtypeStruct((B, S, D), F32), jax.ShapeDtypeStruct((B, S, LANES), F32)),
        grid=(B, S // tm),
        in_specs=[pl.BlockSpec((None, tm, D), row), pl.BlockSpec((None, 1, D), vec),
                  pl.BlockSpec((None, 1, D), vec), pl.BlockSpec((D, LANES), lambda b, i: (0, 0))],
        out_specs=(pl.BlockSpec((None, tm, D), row), pl.BlockSpec((None, tm, LANES), row)),
        compiler_params=_cparams("parallel", "parallel"),
        name="moe_route",
    )(x, sc, sh, router_pad)


GATHER_ROWS = 512


def _moe_gather_kernel(cur_ref, nxt_ref, h_ref, o_ref, buf_ref, sem):
    i, n = pl.program_id(0), pl.num_programs(0)

    def issue(idx_ref, slot):
        def body(g, carry):
            for u in range(2):
                r = 2 * g + u
                pltpu.make_async_copy(h_ref.at[pl.ds(idx_ref[0, 0, r], 1)], buf_ref.at[slot, pl.ds(r, 1)],
                                      sem.at[slot]).start(priority=u)
            return carry
        lax.fori_loop(0, GATHER_ROWS // 2, body, 0, unroll=4)

    @pl.when(i == 0)
    def _():
        issue(cur_ref, 0)

    @pl.when(i + 1 < n)
    def _():
        issue(nxt_ref, (i + 1) % 2)

    slot = i % 2
    pltpu.make_async_copy(h_ref.at[pl.ds(0, GATHER_ROWS)], buf_ref.at[slot], sem.at[slot]).wait()
    o_ref[...] = buf_ref[slot].astype(BF16)


def _moe_gather(h2d, slot_tok):
    T, D = h2d.shape
    n_blocks = slot_tok.shape[0] // GATHER_ROWS
    idx = slot_tok.reshape(n_blocks, 1, GATHER_ROWS)
    return pl.pallas_call(
        _moe_gather_kernel,
        out_shape=jax.ShapeDtypeStruct((n_blocks * GATHER_ROWS, D), BF16),
        grid=(n_blocks,),
        in_specs=[pl.BlockSpec((1, 1, GATHER_ROWS), lambda i: (i, 0, 0), memory_space=pltpu.SMEM),
                  pl.BlockSpec((1, 1, GATHER_ROWS), lambda i: (jnp.minimum(i + 1, n_blocks - 1), 0, 0),
                               memory_space=pltpu.SMEM),
                  pl.BlockSpec(memory_space=pl.ANY)],
        out_specs=pl.BlockSpec((GATHER_ROWS, D), lambda i: (i, 0)),
        scratch_shapes=[pltpu.VMEM((2, GATHER_ROWS, D), F32), pltpu.SemaphoreType.DMA((2,))],
        compiler_params=_cparams("arbitrary"),
        name="moe_gather",
    )(idx, idx, h2d)


def _moe_up_kernel(be_ref, x_ref, wg_ref, wu_ref, o_ref, wg_bf, wu_bf):
    b = pl.program_id(1)
    prev = be_ref[jnp.maximum(b - 1, 0)]

    @pl.when((b == 0) | (be_ref[b] != prev))
    def _():
        wg_bf[...] = wg_ref[...].astype(BF16)
        wu_bf[...] = wu_ref[...].astype(BF16)

    x = x_ref[...]
    a = _silu(jnp.dot(x, wg_bf[...], preferred_element_type=F32)) \
        * jnp.dot(x, wu_bf[...], preferred_element_type=F32)
    o_ref[...] = a.astype(BF16)


def _moe_up(x_slots, block_e, w_gate, w_up, layer):
    slots, D = x_slots.shape
    F = w_gate.shape[-1]
    tf = 896
    n_blocks = slots // MOE_BLOCK
    wspec = pl.BlockSpec((None, None, D, tf), lambda f, b, be: (layer, be[b], 0, f))
    return pl.pallas_call(
        _moe_up_kernel,
        out_shape=jax.ShapeDtypeStruct((slots, F), BF16),
        grid_spec=pltpu.PrefetchScalarGridSpec(
            num_scalar_prefetch=1,
            grid=(F // tf, n_blocks),
            in_specs=[pl.BlockSpec((MOE_BLOCK, D), lambda f, b, be: (b, 0)), wspec, wspec],
            out_specs=pl.BlockSpec((MOE_BLOCK, tf), lambda f, b, be: (b, f)),
            scratch_shapes=[pltpu.VMEM((D, tf), BF16), pltpu.VMEM((D, tf), BF16)]),
        compiler_params=_cparams("arbitrary", "arbitrary"),
        name="moe_gate_up",
    )(block_e, x_slots, w_gate, w_up)


def _moe_down_kernel(be_ref, a_ref, wd_ref, o_ref, wd_bf):
    b = pl.program_id(1)
    prev = be_ref[jnp.maximum(b - 1, 0)]

    @pl.when((b == 0) | (be_ref[b] != prev))
    def _():
        wd_bf[...] = wd_ref[...].astype(BF16)

    o_ref[...] = jnp.dot(a_ref[...], wd_bf[...], preferred_element_type=F32)


def _moe_down(a_slots, block_e, w_down, layer):
    slots, F = a_slots.shape
    D = w_down.shape[-1]
    tn = 512
    n_blocks = slots // MOE_BLOCK
    return pl.pallas_call(
        _moe_down_kernel,
        out_shape=jax.ShapeDtypeStruct((slots, D), F32),
        grid_spec=pltpu.PrefetchScalarGridSpec(
            num_scalar_prefetch=1,
            grid=(D // tn, n_blocks),
            in_specs=[pl.BlockSpec((MOE_BLOCK, F), lambda n, b, be: (b, 0)),
                      pl.BlockSpec((None, None, F, tn), lambda n, b, be: (layer, be[b], 0, n))],
            out_specs=pl.BlockSpec((MOE_BLOCK, tn), lambda n, b, be: (b, n)),
            scratch_shapes=[pltpu.VMEM((F, tn), BF16)]),
        compiler_params=_cparams("arbitrary", "arbitrary"),
        name="moe_down",
    )(block_e, a_slots, w_down)


def _moe_combine_kernel(cur_ref, nxt_ref, x_ref, g2_ref, top_ref, fw_ref, y_ref, o_ref, buf_ref, sem, *,
                        tm, final):
    i = pl.program_id(0) * pl.num_programs(1) + pl.program_id(1)
    n = pl.num_programs(0) * pl.num_programs(1)

    def issue(idx_ref, slot):
        def body(r, carry):
            for k in range(TOP_K):
                pltpu.make_async_copy(y_ref.at[pl.ds(idx_ref[0, 0, TOP_K * r + k], 1)],
                                      buf_ref.at[slot, k, pl.ds(r, 1)], sem.at[slot]).start(priority=k)
            return carry
        lax.fori_loop(0, tm, body, 0, unroll=4)

    @pl.when(i == 0)
    def _():
        issue(cur_ref, 0)

    @pl.when(i + 1 < n)
    def _():
        issue(nxt_ref, (i + 1) % 2)

    slot = i % 2
    for k in range(TOP_K):
        pltpu.make_async_copy(y_ref.at[pl.ds(0, tm)], buf_ref.at[slot, k], sem.at[slot]).wait()
    top = top_ref[...]
    f = top[:, 2:3] * buf_ref[slot, 0] + top[:, 3:4] * buf_ref[slot, 1]
    res = x_ref[...] + g2_ref[...] * f
    o_ref[...] = _rms(res) * fw_ref[...] if final else res


def _moe_combine(x, g2, top, dest, y_slots, final_w, final):
    B, S, D = x.shape
    tm = min(S, 512)
    nb = S // tm
    row = lambda b, i: (b, i, 0)
    idx = dest.reshape(B * nb, 1, TOP_K * tm)
    return pl.pallas_call(
        functools.partial(_moe_combine_kernel, tm=tm, final=final),
        out_shape=jax.ShapeDtypeStruct((B, S, D), F32),
        grid=(B, nb),
        in_specs=[pl.BlockSpec((1, 1, TOP_K * tm), lambda b, i: (b * nb + i, 0, 0), memory_space=pltpu.SMEM),
                  pl.BlockSpec((1, 1, TOP_K * tm), lambda b, i: (jnp.minimum(b * nb + i + 1, B * nb - 1), 0, 0),
                               memory_space=pltpu.SMEM),
                  pl.BlockSpec((None, tm, D), row),
                  pl.BlockSpec((None, 1, D), lambda b, i: (b, 0, 0)),
                  pl.BlockSpec((None, tm, LANES), row),
                  pl.BlockSpec((1, D), lambda b, i: (0, 0)),
                  pl.BlockSpec(memory_space=pl.ANY)],
        out_specs=pl.BlockSpec((None, tm, D), row),
        scratch_shapes=[pltpu.VMEM((2, TOP_K, tm, D), F32), pltpu.SemaphoreType.DMA((2,))],
        compiler_params=_cparams("arbitrary", "arbitrary"),
        name="moe_combine",
    )(idx, idx, x, g2, top, final_w.reshape(1, D), y_slots)


def _moe_ffn(x, sc, sh, g2, router, w_gate, w_up, w_down, layer, final_w, final):
    B, S, D = x.shape
    T = B * S
    router_pad = jnp.zeros((D, LANES), F32).at[:, :N_EXPERTS].set(router)
    h, top = _moe_pre(x, sc, sh, router_pad)
    top2d = top.reshape(T, LANES)
    flat_e = top2d[:, :TOP_K].astype(jnp.int32).reshape(-1)
    n_assign = T * TOP_K
    n_blocks = (n_assign + N_EXPERTS * (MOE_BLOCK - 1)) // MOE_BLOCK
    slots = n_blocks * MOE_BLOCK
    onehot = (flat_e[:, None] == jnp.arange(N_EXPERTS, dtype=jnp.int32)[None, :]).astype(jnp.int32)
    csum = jnp.cumsum(onehot, axis=0)
    rank = jnp.sum(csum * onehot, axis=1) - 1
    counts = csum[-1]
    padded = (counts + MOE_BLOCK - 1) // MOE_BLOCK * MOE_BLOCK
    pad_end = jnp.cumsum(padded)
    start_padded = pad_end - padded
    dest = (start_padded[flat_e] + rank).astype(jnp.int32)
    block_start = jnp.arange(n_blocks, dtype=jnp.int32) * MOE_BLOCK
    block_e = jnp.minimum(jnp.sum((block_start[:, None] >= pad_end[None, :]).astype(jnp.int32), axis=1),
                          N_EXPERTS - 1).astype(jnp.int32)
    flat_tok = jnp.repeat(jnp.arange(T, dtype=jnp.int32), TOP_K)
    slot_tok = (jnp.arange(slots, dtype=jnp.int32) % T).at[dest].set(flat_tok)
    x_slots = _moe_gather(h.reshape(T, D), slot_tok)
    a_slots = _moe_up(x_slots, block_e, w_gate, w_up, layer)
    y_slots = _moe_down(a_slots, block_e, w_down, layer)
    return _moe_combine(x, g2, top, dest, y_slots, final_w, final)


def _final_norm_kernel(x_ref, w_ref, o_ref):
    o_ref[...] = _rms(x_ref[...]) * w_ref[...]


def _final_norm(x2d, w):
    T, D = x2d.shape
    tm = min(T, 1024)
    return pl.pallas_call(
        _final_norm_kernel,
        out_shape=jax.ShapeDtypeStruct((T, D), F32),
        grid=(T // tm,),
        in_specs=[pl.BlockSpec((tm, D), lambda i: (i, 0)), pl.BlockSpec((1, D), lambda i: (0, 0))],
        out_specs=pl.BlockSpec((tm, D), lambda i: (i, 0)),
        compiler_params=_cparams("parallel"),
        name="final_norm",
    )(x2d, w.reshape(1, D))


def _arrange_w_in(w):
    D = w.shape[0]
    o = np.cumsum([0, 3 * GDN_DIM, GDN_DIM, GDN_HEADS, GDN_HEADS, SSM_INNER, SSM_CONV_DIM, SSM_HEADS,
                   MLA_Q_LORA, MLA_KV_LORA, MLA_ROPE, 3 * w.shape[0]])
    seg = lambda i: w[:, int(o[i]):int(o[i + 1])]
    qkv, gz, ga, gb, sz, xbc, dt, cq, ckv, kr, gates = [seg(i) for i in range(11)]
    half = MLA_ROPE // 2
    small = jnp.concatenate([ga, gb, dt, jnp.zeros((D, LANES - 2 * GDN_HEADS - SSM_HEADS), w.dtype)], axis=1)
    kr_full = jnp.concatenate([kr, -kr[:, half:], kr[:, :half]], axis=1)
    out = jnp.concatenate([qkv, gz, sz, cq, xbc, gates, ckv, small, kr_full], axis=1)
    assert out.shape[1] == IN_COLS
    return out.astype(BF16)


def _arrange_w_uq(w):
    half = MLA_ROPE // 2
    parts = []
    for h in range(MLA_HEADS):
        c0 = h * (MLA_NOPE + MLA_ROPE)
        rope = w[:, c0 + MLA_NOPE:c0 + MLA_NOPE + MLA_ROPE]
        parts += [w[:, c0:c0 + MLA_NOPE], rope, -rope[:, half:], rope[:, :half]]
    return jnp.concatenate(parts, axis=1).astype(BF16)


def kernel(x, c, positions, w_ada, b_ada, w_in, gdn_conv_w, gdn_a_log, gdn_dt_bias, gdn_norm_w, ssm_conv_w, ssm_conv_b, ssm_a_log, ssm_dt_bias, ssm_d, ssm_norm_w, mla_q_norm_w, mla_w_uq, mla_kv_norm_w, mla_w_uk, mla_w_uv, w_branch_a, w_branch_b, w_branch_c, w_out, ffn_w_gate, ffn_w_up, ffn_w_down, moe_router, moe_w_gate, moe_w_up, moe_w_down, final_norm_w):
    B, S, D = x.shape
    T = B * S
    depth = w_ada.shape[0]
    c_pad = jnp.zeros((SUBLANES, D), F32).at[:B].set(c)
    mod = _modulation(c_pad, w_ada, b_ada)[:, :B].reshape(depth, B, 6, 1, D)
    cos, sin = _rope_tables(positions)
    for l in range(depth):
        sh1, sc1, g1, sh2, sc2, g2 = [mod[l, :, n] for n in range(6)]
        proj, small = _in_proj(x, sc1, sh1, _arrange_w_in(w_in[l]))
        y_a = _gdn(proj, small, gdn_conv_w[l], gdn_a_log[l], gdn_dt_bias[l], gdn_norm_w[l])
        y_b = _ssd(proj, small, ssm_conv_w[l], ssm_conv_b[l], ssm_a_log[l], ssm_dt_bias[l], ssm_d[l], ssm_norm_w[l])
        q, k, v = _mla_proj(proj.reshape(T, IN_COLS), cos, sin, mla_q_norm_w[l], _arrange_w_uq(mla_w_uq[l]),
                            mla_kv_norm_w[l], mla_w_uk[l].astype(BF16), mla_w_uv[l].astype(BF16))
        y_c = _attention(q.reshape(B, S, -1), k.reshape(B, S, -1), v.reshape(B, S, -1))
        x = _merge(y_a, y_b, y_c, proj, x, g1, w_branch_a[l].astype(BF16), w_branch_b[l].astype(BF16),
                   w_branch_c[l].astype(BF16), w_out[l].astype(BF16))
        i = l // 2
        if l % 2 == 0:
            x = _dense_ffn(x, sc2, sh2, g2, ffn_w_gate, ffn_w_up, ffn_w_down, i)
        else:
            x = _moe_ffn(x, sc2, sh2, g2, moe_router[i], moe_w_gate, moe_w_up, moe_w_down, i,
                         final_norm_w, l == depth - 1)
    if depth % 2 == 0:
        return x
    return _final_norm(x.reshape(T, D), final_norm_w).reshape(B, S, D)
```
